```python
import math
import jax, jax.numpy as jnp
from jax import lax
import numpy as np

D_MODEL = 1024
BATCH = 16
SEQ = 2048
DEPTH = 2

N_MIXERS = 2
ROPE_THETA = 500000.0
NORM_EPS = 1e-6
Q_BLOCK = 128

MLA_HEADS = 16
MLA_Q_LORA = 384
MLA_KV_LORA = 256
MLA_NOPE = 64
MLA_ROPE = 32
MLA_V = 64
MLA_QK = MLA_NOPE + MLA_ROPE
MLA_DOWN = MLA_Q_LORA + MLA_KV_LORA + MLA_ROPE

DIFF_HEADS = 8
DIFF_HEAD_DIM = 64
DIFF_V = 2 * DIFF_HEAD_DIM
DIFF_ROPE = DIFF_HEAD_DIM // 4
DIFF_QK_WIDTH = DIFF_HEADS * 2 * DIFF_HEAD_DIM
DIFF_V_WIDTH = DIFF_HEADS * DIFF_V
DIFF_QKV = 2 * DIFF_QK_WIDTH + DIFF_V_WIDTH

D_FF = 2816
CONV_WIDTH = 3

N_MLA_LAYERS = (DEPTH + N_MIXERS - 1) // N_MIXERS
N_DIFF_LAYERS = DEPTH // N_MIXERS

kernel_name = 'hybrid_mla_diffattn_convffn_encoder'


def rmsnorm(x, g=None):
    x32 = x.astype(jnp.float32)
    y = x32 * lax.rsqrt(jnp.mean(x32 * x32, axis=-1, keepdims=True) + NORM_EPS)
    if g is not None:
        y = y * g.astype(jnp.float32)
    return y.astype(x.dtype)


def rope_cos_sin(positions, rot_dim):
    inv_freq = ROPE_THETA ** (-jnp.arange(0, rot_dim, 2, dtype=jnp.float32) / rot_dim)
    ang = positions.astype(jnp.float32)[..., None] * inv_freq
    return jnp.cos(ang), jnp.sin(ang)


def apply_rope(x, cos, sin):
    extra = x.ndim - cos.ndim
    shp = cos.shape[:2] + (1,) * extra + cos.shape[-1:]
    c, s = cos.reshape(shp), sin.reshape(shp)
    half = x.shape[-1] // 2
    x32 = x.astype(jnp.float32)
    x1, x2 = x32[..., :half], x32[..., half:]
    return jnp.concatenate([x1 * c - x2 * s, x2 * c + x1 * s], axis=-1).astype(x.dtype)


def to_blocks(t):
    b, s = t.shape[:2]
    t = t.reshape((b, s // Q_BLOCK, Q_BLOCK) + t.shape[2:])
    return jnp.moveaxis(t, 1, 0)


def from_blocks(t):
    t = jnp.moveaxis(t, 0, 1)
    return t.reshape((t.shape[0], t.shape[1] * t.shape[2]) + t.shape[3:])


def mla_mixer(h, cos, sin, w_down, q_norm_g, kv_norm_g, w_uq, w_ukv, w_o):
    b, s, _ = h.shape
    down = h @ w_down
    c_q, c_kv, k_rope = jnp.split(down, [MLA_Q_LORA, MLA_Q_LORA + MLA_KV_LORA], axis=-1)
    q = (rmsnorm(c_q, q_norm_g) @ w_uq).reshape(b, s, MLA_HEADS, MLA_QK)
    q_nope = q[..., :MLA_NOPE]
    q_rope = apply_rope(q[..., MLA_NOPE:], cos, sin)
    kv = (rmsnorm(c_kv, kv_norm_g) @ w_ukv).reshape(b, s, MLA_HEADS, MLA_NOPE + MLA_V)
    k_nope, v = kv[..., :MLA_NOPE], kv[..., MLA_NOPE:]
    k_rope = apply_rope(k_rope, cos, sin)
    scale = MLA_QK ** -0.5

    def block(qb):
        qn, qr = qb
        sc = (jnp.einsum('bqhd,bkhd->bhqk', qn, k_nope)
              + jnp.einsum('bqhr,bkr->bhqk', qr, k_rope))
        p = jax.nn.softmax(sc.astype(jnp.float32) * scale, axis=-1).astype(v.dtype)
        return jnp.einsum('bhqk,bkhd->bqhd', p, v)

    o = from_blocks(lax.map(block, (to_blocks(q_nope), to_blocks(q_rope))))
    return o.reshape(b, s, MLA_HEADS * MLA_V) @ w_o


def diff_mixer(h, cos, sin, layer_idx, w_qkv, lam_q1, lam_k1, lam_q2, lam_k2, w_o):
    b, s, _ = h.shape
    q, k, v = jnp.split(h @ w_qkv, [DIFF_QK_WIDTH, 2 * DIFF_QK_WIDTH], axis=-1)
    q = q.reshape(b, s, DIFF_HEADS, 2, DIFF_HEAD_DIM)
    k = k.reshape(b, s, DIFF_HEADS, 2, DIFF_HEAD_DIM)
    v = v.reshape(b, s, DIFF_HEADS, DIFF_V)
    q = jnp.concatenate([apply_rope(q[..., :DIFF_ROPE], cos, sin), q[..., DIFF_ROPE:]], axis=-1)
    k = jnp.concatenate([apply_rope(k[..., :DIFF_ROPE], cos, sin), k[..., DIFF_ROPE:]], axis=-1)
    lam_init = 0.8 - 0.6 * math.exp(-0.3 * layer_idx)
    f32 = jnp.float32
    lam = (jnp.exp(jnp.sum(lam_q1.astype(f32) * lam_k1.astype(f32)))
           - jnp.exp(jnp.sum(lam_q2.astype(f32) * lam_k2.astype(f32))) + lam_init)
    scale = DIFF_HEAD_DIM ** -0.5

    def block(qb):
        sc = jnp.einsum('bqhcd,bkhcd->bhcqk', qb, k).astype(f32) * scale
        p = jax.nn.softmax(sc, axis=-1)
        a = (p[:, :, 0] - lam * p[:, :, 1]).astype(v.dtype)
        return jnp.einsum('bhqk,bkhe->bqhe', a, v)

    o = from_blocks(lax.map(block, to_blocks(q)))
    o = (rmsnorm(o) * (1.0 - lam_init)).astype(h.dtype)
    return o.reshape(b, s, DIFF_V_WIDTH) @ w_o


def conv_ffn(h, w_up, conv_w, conv_b, w_down):
    u = h @ w_up
    pad = CONV_WIDTH // 2
    u = lax.conv_general_dilated(
        u, conv_w[:, None, :].astype(u.dtype), window_strides=(1,), padding=((pad, pad),),
        dimension_numbers=('NWC', 'WIO', 'NWC'), feature_group_count=2 * D_FF) + conv_b
    g, val = jnp.split(u, 2, axis=-1)
    return (jax.nn.silu(g) * val) @ w_down


def setup_inputs(seed: int = 0) -> dict:
    key = jax.random.key(seed)
    ks = jax.random.split(key, 24)
    f32 = jnp.float32

    def dense(k, lead, fan_in, fan_out):
        return jax.random.normal(k, lead + (fan_in, fan_out), f32) * fan_in ** -0.5

    def gain(k, shape):
        return 1.0 + 0.02 * jax.random.normal(k, shape, f32)

    na, nb = (N_MLA_LAYERS,), (N_DIFF_LAYERS,)
    return {
        'x': jax.random.normal(ks[0], (BATCH, SEQ, D_MODEL), f32),
        'positions': jnp.broadcast_to(jnp.arange(SEQ, dtype=jnp.int32), (BATCH, SEQ)),
        'attn_norm_g': gain(ks[1], (DEPTH, D_MODEL)),
        'ffn_norm_g': gain(ks[2], (DEPTH, D_MODEL)),
        'final_norm_g': gain(ks[3], (D_MODEL,)),
        'mla_w_down': dense(ks[4], na, D_MODEL, MLA_DOWN),
        'mla_q_norm_g': gain(ks[5], na + (MLA_Q_LORA,)),
        'mla_kv_norm_g': gain(ks[6], na + (MLA_KV_LORA,)),
        'mla_w_uq': dense(ks[7], na, MLA_Q_LORA, MLA_HEADS * MLA_QK),
        'mla_w_ukv': dense(ks[8], na, MLA_KV_LORA, MLA_HEADS * (MLA_NOPE + MLA_V)),
        'mla_w_o': dense(ks[9], na, MLA_HEADS * MLA_V, D_MODEL),
        'diff_w_qkv': dense(ks[10], nb, D_MODEL, DIFF_QKV),
        'diff_lam_q1': 0.1 * jax.random.normal(ks[11], nb + (DIFF_HEAD_DIM,), f32),
        'diff_lam_k1': 0.1 * jax.random.normal(ks[12], nb + (DIFF_HEAD_DIM,), f32),
        'diff_lam_q2': 0.1 * jax.random.normal(ks[13], nb + (DIFF_HEAD_DIM,), f32),
        'diff_lam_k2': 0.1 * jax.random.normal(ks[14], nb + (DIFF_HEAD_DIM,), f32),
        'diff_w_o': dense(ks[15], nb, DIFF_V_WIDTH, D_MODEL),
        'ffn_w_up': dense(ks[16], (DEPTH,), D_MODEL, 2 * D_FF),
        'ffn_conv_w': jax.random.normal(ks[17], (DEPTH, CONV_WIDTH, 2 * D_FF), f32) * CONV_WIDTH ** -0.5,
        'ffn_conv_b': 0.02 * jax.random.normal(ks[18], (DEPTH, 2 * D_FF), f32),
        'ffn_w_down': dense(ks[19], (DEPTH,), D_FF, D_MODEL),
    }


def reference(x, positions, attn_norm_g, ffn_norm_g, final_norm_g,
              mla_w_down, mla_q_norm_g, mla_kv_norm_g, mla_w_uq, mla_w_ukv, mla_w_o,
              diff_w_qkv, diff_lam_q1, diff_lam_k1, diff_lam_q2, diff_lam_k2, diff_w_o,
              ffn_w_up, ffn_conv_w, ffn_conv_b, ffn_w_down):
    cos_a, sin_a = rope_cos_sin(positions, MLA_ROPE)
    cos_b, sin_b = rope_cos_sin(positions, DIFF_ROPE)
    for i in range(DEPTH):
        h = rmsnorm(x, attn_norm_g[i])
        j = i // N_MIXERS
        if i % N_MIXERS == 0:
            x = x + mla_mixer(h, cos_a, sin_a, mla_w_down[j], mla_q_norm_g[j], mla_kv_norm_g[j],
                              mla_w_uq[j], mla_w_ukv[j], mla_w_o[j])
        else:
            x = x + diff_mixer(h, cos_b, sin_b, i, diff_w_qkv[j], diff_lam_q1[j], diff_lam_k1[j],
                               diff_lam_q2[j], diff_lam_k2[j], diff_w_o[j])
        x = x + conv_ffn(rmsnorm(x, ffn_norm_g[i]), ffn_w_up[i], ffn_conv_w[i], ffn_conv_b[i],
                         ffn_w_down[i])
    return rmsnorm(x, final_norm_g)
```

```python
import functools
import math

import jax
import jax.numpy as jnp
from jax import lax
from jax.experimental import pallas as pl
from jax.experimental.pallas import tpu as pltpu

F32 = jnp.float32
BF16 = jnp.bfloat16

D_MODEL = 1024
ROPE_THETA = 500000.0
NORM_EPS = 1e-6

MLA_HEADS = 16
MLA_Q_LORA = 384
MLA_KV_LORA = 256
MLA_NOPE = 64
MLA_ROPE = 32
MLA_V = 64
MLA_QK = MLA_NOPE + MLA_ROPE

DIFF_HEADS = 8
DIFF_HEAD_DIM = 64
DIFF_V = 2 * DIFF_HEAD_DIM
DIFF_ROPE = DIFF_HEAD_DIM // 4
DIFF_QK_WIDTH = DIFF_HEADS * 2 * DIFF_HEAD_DIM
DIFF_V_WIDTH = DIFF_HEADS * DIFF_V

D_FF = 2816
CONV_WIDTH = 3

LANES = 128
BF16_SUBLANES = 16
LOG2E = math.log2(math.e)

TOKEN_TILE = 512
Q_TILE = 256
FF_CHUNK = 256
VMEM_LIMIT = 48 * 1024 * 1024


def _rms(x):
    return x * lax.rsqrt(jnp.mean(x * x, axis=-1, keepdims=True) + NORM_EPS)


def _apply_rope(x, cos, sin_lo, sin_hi, half):
    return (x * cos + pltpu.roll(x, LANES - half, 1) * sin_lo
            + pltpu.roll(x, half, 1) * sin_hi)


def _mla_proj_kernel(x_ref, pos_ref, invf_ref, g_ref, wd_ref, qg_ref, kvg_ref, wuq_ref, wuk_ref,
                     wuv_ref, q_ref, k_ref, v_ref):
    h = _rms(x_ref[...]) * g_ref[...]
    down = jnp.dot(h.astype(BF16), wd_ref[...], preferred_element_type=F32)
    c_q = down[:, :MLA_Q_LORA]
    c_kv = down[:, MLA_Q_LORA:MLA_Q_LORA + MLA_KV_LORA]
    k_rope = down[:, MLA_Q_LORA + MLA_KV_LORA:]
    cqn = (_rms(c_q) * qg_ref[...]).astype(BF16)
    ckvn = (_rms(c_kv) * kvg_ref[...]).astype(BF16)
    q = jnp.dot(cqn, wuq_ref[...], preferred_element_type=F32)
    kn = jnp.dot(ckvn, wuk_ref[...], preferred_element_type=F32)
    v_ref[...] = jnp.dot(ckvn, wuv_ref[...], preferred_element_type=F32).astype(BF16)

    ang = pos_ref[...].astype(F32) * invf_ref[...]
    cos = jnp.cos(ang)
    sin = jnp.sin(ang)
    lane = lax.broadcasted_iota(jnp.int32, ang.shape, 1)
    half = MLA_ROPE // 2
    sin_lo = jnp.where(lane < MLA_NOPE + half, -sin, 0.0)
    sin_hi = jnp.where(lane >= MLA_NOPE + half, sin, 0.0)
    kr = _apply_rope(k_rope, cos, sin_lo, sin_hi, half)
    qs = (MLA_QK ** -0.5) * LOG2E
    cos_q, sin_lo_q, sin_hi_q = cos * qs, sin_lo * qs, sin_hi * qs
    for hd in range(MLA_HEADS):
        blk = slice(hd * LANES, (hd + 1) * LANES)
        q_ref[:, blk] = _apply_rope(q[:, blk], cos_q, sin_lo_q, sin_hi_q, half).astype(BF16)
        k_ref[:, blk] = (kn[:, blk] + kr).astype(BF16)


def _mla_proj(x2, pos2, invf, g, wd, qg, kvg, wuq, wuk, wuv):
    t = x2.shape[0]
    tm = TOKEN_TILE
    full = lambda a: pl.BlockSpec(a.shape, lambda i: (0, 0))
    return pl.pallas_call(
        _mla_proj_kernel,
        grid=(t // tm,),
        in_specs=[pl.BlockSpec((tm, D_MODEL), lambda i: (i, 0)),
                  pl.BlockSpec((tm, 1), lambda i: (i, 0)),
                  full(invf), full(g), full(wd), full(qg), full(kvg), full(wuq), full(wuk),
                  full(wuv)],
        out_specs=[pl.BlockSpec((tm, MLA_HEADS * LANES), lambda i: (i, 0)),
                   pl.BlockSpec((tm, MLA_HEADS * LANES), lambda i: (i, 0)),
                   pl.BlockSpec((tm, MLA_HEADS * MLA_V), lambda i: (i, 0))],
        out_shape=[jax.ShapeDtypeStruct((t, MLA_HEADS * LANES), BF16),
                   jax.ShapeDtypeStruct((t, MLA_HEADS * LANES), BF16),
                   jax.ShapeDtypeStruct((t, MLA_HEADS * MLA_V), BF16)],
        compiler_params=pltpu.CompilerParams(dimension_semantics=("parallel",),
                                             vmem_limit_bytes=VMEM_LIMIT),
        name="mla_proj",
    )(x2, pos2, invf, g, wd, qg, kvg, wuq, wuk, wuv)


def _mla_attn_kernel(q_ref, k_ref, v_ref, o_ref):
    v = v_ref[...]
    outs = []
    for hh in range(2):
        blk = slice(hh * LANES, (hh + 1) * LANES)
        s = lax.dot_general(q_ref[:, blk], k_ref[:, blk], (((1,), (1,)), ((), ())),
                            preferred_element_type=F32)
        m = jnp.max(s, axis=-1, keepdims=True)
        p = jnp.exp2(s - m)
        l = jnp.sum(p, axis=-1, keepdims=True)
        outs.append(jnp.dot(p.astype(BF16), v, preferred_element_type=F32) / l)
    lane = lax.broadcasted_iota(jnp.int32, outs[0].shape, 1)
    o_ref[...] = jnp.where(lane < MLA_V, outs[0], outs[1]).astype(BF16)


def _mla_attn(q, k, v, batch, seq):
    t = q.shape[0]
    tq = Q_TILE
    nq = seq // tq
    return pl.pallas_call(
        _mla_attn_kernel,
        grid=(batch, MLA_HEADS // 2, nq),
        in_specs=[pl.BlockSpec((tq, 2 * LANES), lambda b, h, i: (b * nq + i, h)),
                  pl.BlockSpec((seq, 2 * LANES), lambda b, h, i: (b, h)),
                  pl.BlockSpec((seq, 2 * MLA_V), lambda b, h, i: (b, h))],
        out_specs=pl.BlockSpec((tq, 2 * MLA_V), lambda b, h, i: (b * nq + i, h)),
        out_shape=jax.ShapeDtypeStruct((t, MLA_HEADS * MLA_V), BF16),
        compiler_params=pltpu.CompilerParams(
            dimension_semantics=("parallel", "parallel", "parallel"),
            vmem_limit_bytes=VMEM_LIMIT),
        name="mla_attn",
    )(q, k, v)


def _diff_proj_kernel(x_ref, pos_ref, invf_ref, g_ref, w_ref, q_ref, k_ref, v_ref):
    h = (_rms(x_ref[...]) * g_ref[...]).astype(BF16)
    qkv = jnp.dot(h, w_ref[...], preferred_element_type=F32)
    v_ref[...] = qkv[:, 2 * DIFF_QK_WIDTH:].astype(BF16)

    ang = pos_ref[...].astype(F32) * invf_ref[...]
    cos = jnp.cos(ang)
    sin = jnp.sin(ang)
    lane = lax.broadcasted_iota(jnp.int32, ang.shape, 1)
    half = DIFF_ROPE // 2
    sin_lo = jnp.where(lane % DIFF_HEAD_DIM < half, -sin, 0.0)
    sin_hi = jnp.where(lane % DIFF_HEAD_DIM >= half, sin, 0.0)
    qs = (DIFF_HEAD_DIM ** -0.5) * LOG2E
    cos_q, sin_lo_q, sin_hi_q = cos * qs, sin_lo * qs, sin_hi * qs
    for hd in range(DIFF_HEADS):
        blk = slice(hd * LANES, (hd + 1) * LANES)
        kblk = slice(DIFF_QK_WIDTH + hd * LANES, DIFF_QK_WIDTH + (hd + 1) * LANES)
        q_ref[:, blk] = _apply_rope(qkv[:, blk], cos_q, sin_lo_q, sin_hi_q, half).astype(BF16)
        k_ref[:, blk] = _apply_rope(qkv[:, kblk], cos, sin_lo, sin_hi, half).astype(BF16)


def _diff_proj(x2, pos2, invf, g, w):
    t = x2.shape[0]
    tm = TOKEN_TILE
    full = lambda a: pl.BlockSpec(a.shape, lambda i: (0, 0))
    row = lambda n: pl.BlockSpec((tm, n), lambda i: (i, 0))
    return pl.pallas_call(
        _diff_proj_kernel,
        grid=(t // tm,),
        in_specs=[row(D_MODEL), row(1), full(invf), full(g), full(w)],
        out_specs=[row(DIFF_QK_WIDTH), row(DIFF_QK_WIDTH), row(DIFF_V_WIDTH)],
        out_shape=[jax.ShapeDtypeStruct((t, DIFF_QK_WIDTH), BF16),
                   jax.ShapeDtypeStruct((t, DIFF_QK_WIDTH), BF16),
                   jax.ShapeDtypeStruct((t, DIFF_V_WIDTH), BF16)],
        compiler_params=pltpu.CompilerParams(dimension_semantics=("parallel",),
                                             vmem_limit_bytes=VMEM_LIMIT),
        name="diff_proj",
    )(x2, pos2, invf, g, w)


def _diff_attn_kernel(lam_ref, q_ref, k_ref, v_ref, o_ref, *, lam_init):
    lam_p = lam_ref[...]
    lam = (jnp.exp(jnp.sum(lam_p[0:1] * lam_p[1:2], axis=-1, keepdims=True))
           - jnp.exp(jnp.sum(lam_p[2:3] * lam_p[3:4], axis=-1, keepdims=True)) + lam_init)
    q = q_ref[...]
    k = k_ref[...]
    v = v_ref[...]
    lane = lax.broadcasted_iota(jnp.int32, q.shape, 1)
    zero = jnp.zeros_like(q)
    outs = []
    for comp in range(2):
        in_comp = (lane < DIFF_HEAD_DIM) if comp == 0 else (lane >= DIFF_HEAD_DIM)
        qc = jnp.where(in_comp, q, zero)
        s = lax.dot_general(qc, k, (((1,), (1,)), ((), ())), preferred_element_type=F32)
        m = jnp.max(s, axis=-1, keepdims=True)
        p = jnp.exp2(s - m)
        l = jnp.sum(p, axis=-1, keepdims=True)
        outs.append(jnp.dot(p.astype(BF16), v, preferred_element_type=F32) / l)
    o = outs[0] - lam * outs[1]
    o_ref[...] = (_rms(o) * (1.0 - lam_init)).astype(BF16)


def _diff_attn(lam_params, q, k, v, batch, seq, lam_init):
    t = q.shape[0]
    tq = Q_TILE
    nq = seq // tq
    return pl.pallas_call(
        functools.partial(_diff_attn_kernel, lam_init=lam_init),
        grid=(batch, DIFF_HEADS, nq),
        in_specs=[pl.BlockSpec(lam_params.shape, lambda b, h, i: (0, 0)),
                  pl.BlockSpec((tq, LANES), lambda b, h, i: (b * nq + i, h)),
                  pl.BlockSpec((seq, LANES), lambda b, h, i: (b, h)),
                  pl.BlockSpec((seq, LANES), lambda b, h, i: (b, h))],
        out_specs=pl.BlockSpec((tq, LANES), lambda b, h, i: (b * nq + i, h)),
        out_shape=jax.ShapeDtypeStruct((t, DIFF_V_WIDTH), BF16),
        compiler_params=pltpu.CompilerParams(
            dimension_semantics=("parallel", "parallel", "parallel"),
            vmem_limit_bytes=VMEM_LIMIT),
        name="diff_attn",
    )(lam_params, q, k, v)


def _out_proj_kernel(o_ref, x_ref, w_ref, g_ref, xn_ref, h_ref):
    xn = x_ref[...] + jnp.dot(o_ref[...], w_ref[...], preferred_element_type=F32)
    xn_ref[...] = xn
    h_ref[...] = (_rms(xn) * g_ref[...]).astype(BF16)


def _out_proj(o, x2, w, g):
    t = x2.shape[0]
    tm = TOKEN_TILE
    full = lambda a: pl.BlockSpec(a.shape, lambda i: (0, 0))
    row = lambda n: pl.BlockSpec((tm, n), lambda i: (i, 0))
    return pl.pallas_call(
        _out_proj_kernel,
        grid=(t // tm,),
        in_specs=[row(o.shape[1]), row(D_MODEL), full(w), full(g)],
        out_specs=[row(D_MODEL), row(D_MODEL)],
        out_shape=[jax.ShapeDtypeStruct((t, D_MODEL), F32),
                   jax.ShapeDtypeStruct((t, D_MODEL), BF16)],
        compiler_params=pltpu.CompilerParams(dimension_semantics=("parallel",),
                                             vmem_limit_bytes=VMEM_LIMIT),
        name="out_proj",
    )(o, x2, w, g)


def _ffn_kernel(x_ref, h_ref, hprev_ref, hnext_ref, wup_ref, cw_ref, cb_ref, wdn_ref, fg_ref,
                out_ref, hext_ref, u_ref, acc_ref, *, tiles_per_seq, apply_final_norm):
    i = pl.program_id(0)
    f = pl.program_id(1)
    tm = x_ref.shape[0]
    hb = BF16_SUBLANES
    fc = wdn_ref.shape[0]

    @pl.when(f == 0)
    def _():
        at_start = (i % tiles_per_seq) == 0
        at_end = (i % tiles_per_seq) == tiles_per_seq - 1
        hprev = hprev_ref[...]
        hnext = hnext_ref[...]
        hext_ref[0:hb, :] = jnp.where(at_start, jnp.zeros_like(hprev), hprev)
        hext_ref[hb:hb + tm, :] = h_ref[...]
        hext_ref[hb + tm:, :] = jnp.where(at_end, jnp.zeros_like(hnext), hnext)
        acc_ref[...] = x_ref[...]

    u_ref[...] = jnp.dot(hext_ref[...], wup_ref[...], preferred_element_type=F32)
    cw = cw_ref[...]
    uc = (u_ref[hb - 1:hb - 1 + tm, :] * cw[0:1] + u_ref[hb:hb + tm, :] * cw[1:2]
          + u_ref[hb + 1:hb + 1 + tm, :] * cw[2:3] + cb_ref[...])
    gate = uc[:, :fc]
    val = uc[:, fc:]
    act = (gate * jax.nn.sigmoid(gate) * val).astype(BF16)
    acc_ref[...] += jnp.dot(act, wdn_ref[...], preferred_element_type=F32)

    @pl.when(f == pl.num_programs(1) - 1)
    def _():
        y = acc_ref[...]
        if apply_final_norm:
            y = _rms(y) * fg_ref[...]
        out_ref[...] = y


def _ffn(xn, h, wup, cw, cb, wdn, fg, seq, apply_final_norm):
    t = xn.shape[0]
    tm = TOKEN_TILE
    fc = FF_CHUNK
    hb = BF16_SUBLANES
    nf = D_FF // fc
    halo_per_tile = tm // hb
    last_halo = t // hb - 1
    return pl.pallas_call(
        functools.partial(_ffn_kernel, tiles_per_seq=seq // tm, apply_final_norm=apply_final_norm),
        grid=(t // tm, nf),
        in_specs=[pl.BlockSpec((tm, D_MODEL), lambda i, f: (i, 0)),
                  pl.BlockSpec((tm, D_MODEL), lambda i, f: (i, 0)),
                  pl.BlockSpec((hb, D_MODEL), lambda i, f: (jnp.maximum(i * halo_per_tile - 1, 0), 0)),
                  pl.BlockSpec((hb, D_MODEL),
                               lambda i, f: (jnp.minimum((i + 1) * halo_per_tile, last_halo), 0)),
                  pl.BlockSpec((D_MODEL, 2 * fc), lambda i, f: (0, f)),
                  pl.BlockSpec((CONV_WIDTH, 2 * fc), lambda i, f: (0, f)),
                  pl.BlockSpec((1, 2 * fc), lambda i, f: (0, f)),
                  pl.BlockSpec((fc, D_MODEL), lambda i, f: (f, 0)),
                  pl.BlockSpec((1, D_MODEL), lambda i, f: (0, 0))],
        out_specs=pl.BlockSpec((tm, D_MODEL), lambda i, f: (i, 0)),
        out_shape=jax.ShapeDtypeStruct((t, D_MODEL), F32),
        scratch_shapes=[pltpu.VMEM((tm + 2 * hb, D_MODEL), BF16),
                        pltpu.VMEM((tm + 2 * hb, 2 * fc), F32),
                        pltpu.VMEM((tm, D_MODEL), F32)],
        compiler_params=pltpu.CompilerParams(dimension_semantics=("parallel", "arbitrary"),
                                             vmem_limit_bytes=VMEM_LIMIT),
        name="conv_ffn",
    )(xn, h, h, h, wup, cw, cb, wdn, fg)


def _inv_freq_lanes(rot_dim, period, offset):
    inv = ROPE_THETA ** (-jnp.arange(0, rot_dim, 2, dtype=F32) / rot_dim)
    one = jnp.zeros((period,), F32).at[offset:offset + rot_dim].set(jnp.concatenate([inv, inv]))
    return jnp.tile(one, LANES // period)[None, :]


def _pad_heads(w, heads, width):
    w = w.reshape(w.shape[0], heads, width)
    return jnp.pad(w, ((0, 0), (0, 0), (0, LANES - width))).reshape(w.shape[0], heads * LANES)


def _chunk_gate_val(a, fc):
    lead = a.shape[:-1]
    a = a.reshape(lead + (2, D_FF // fc, fc))
    return jnp.swapaxes(a, -3, -2).reshape(lead + (2 * D_FF,))


def kernel(x, positions, attn_norm_g, ffn_norm_g, final_norm_g, mla_w_down, mla_q_norm_g,
           mla_kv_norm_g, mla_w_uq, mla_w_ukv, mla_w_o, diff_w_qkv, diff_lam_q1, diff_lam_k1,
           diff_lam_q2, diff_lam_k2, diff_w_o, ffn_w_up, ffn_conv_w, ffn_conv_b, ffn_w_down):
    batch, seq, _ = x.shape
    t = batch * seq
    x2 = x.reshape(t, D_MODEL)
    pos2 = positions.reshape(t, 1)
    row = lambda a: a.reshape(1, -1)

    def ffn(xn, h, layer, final):
        return _ffn(xn, h,
                    _chunk_gate_val(ffn_w_up[layer], FF_CHUNK).astype(BF16),
                    _chunk_gate_val(ffn_conv_w[layer], FF_CHUNK),
                    row(_chunk_gate_val(ffn_conv_b[layer], FF_CHUNK)),
                    ffn_w_down[layer].astype(BF16), row(final_norm_g), seq, final)

    wd = mla_w_down[0]
    k_rope_cols = jnp.pad(wd[:, MLA_Q_LORA + MLA_KV_LORA:], ((0, 0), (MLA_NOPE, LANES - MLA_QK)))
    wd_p = jnp.concatenate([wd[:, :MLA_Q_LORA + MLA_KV_LORA], k_rope_cols], axis=1).astype(BF16)
    wukv = mla_w_ukv[0].reshape(MLA_KV_LORA, MLA_HEADS, MLA_NOPE + MLA_V)
    wuk_p = _pad_heads(wukv[:, :, :MLA_NOPE].reshape(MLA_KV_LORA, -1), MLA_HEADS, MLA_NOPE)
    wuv = wukv[:, :, MLA_NOPE:].reshape(MLA_KV_LORA, -1)
    q, k, v = _mla_proj(x2, pos2, _inv_freq_lanes(MLA_ROPE, LANES, MLA_NOPE), row(attn_norm_g[0]),
                        wd_p, row(mla_q_norm_g[0]), row(mla_kv_norm_g[0]),
                        _pad_heads(mla_w_uq[0], MLA_HEADS, MLA_QK).astype(BF16),
                        wuk_p.astype(BF16), wuv.astype(BF16))
    o = _mla_attn(q, k, v, batch, seq)
    xn, h = _out_proj(o, x2, mla_w_o[0].astype(BF16), row(ffn_norm_g[0]))
    x2 = ffn(xn, h, 0, False)

    lam_init = 0.8 - 0.6 * math.exp(-0.3 * 1)
    q, k, v = _diff_proj(x2, pos2, _inv_freq_lanes(DIFF_ROPE, DIFF_HEAD_DIM, 0),
                         row(attn_norm_g[1]), diff_w_qkv[0].astype(BF16))
    lam_params = jnp.stack([diff_lam_q1[0], diff_lam_k1[0], diff_lam_q2[0], diff_lam_k2[0]])
    o = _diff_attn(lam_params, q, k, v, batch, seq, lam_init)
    xn, h = _out_proj(o, x2, diff_w_o[0].astype(BF16), row(ffn_norm_g[1]))
    x2 = ffn(xn, h, 1, True)
    return x2.reshape(batch, seq, D_MODEL)
```

```python
import functools
import math

import jax
import jax.numpy as jnp
from jax import lax
from jax.experimental import pallas as pl
from jax.experimental.pallas import tpu as pltpu

F32 = jnp.float32
BF16 = jnp.bfloat16

D_MODEL = 1024
ROPE_THETA = 500000.0
NORM_EPS = 1e-6

MLA_HEADS = 16
MLA_Q_LORA = 384
MLA_KV_LORA = 256
MLA_NOPE = 64
MLA_ROPE = 32
MLA_V = 64
MLA_QK = MLA_NOPE + MLA_ROPE

DIFF_HEADS = 8
DIFF_HEAD_DIM = 64
DIFF_V = 2 * DIFF_HEAD_DIM
DIFF_ROPE = DIFF_HEAD_DIM // 4
DIFF_QK_WIDTH = DIFF_HEADS * 2 * DIFF_HEAD_DIM
DIFF_V_WIDTH = DIFF_HEADS * DIFF_V

D_FF = 2816
CONV_WIDTH = 3

LANES = 128
BF16_SUBLANES = 16
VT_ONES_ROWS = BF16_SUBLANES
LOG2E = math.log2(math.e)

TOKEN_TILE = 512
Q_TILE = 512
KEY_CHUNK = 256
FF_CHUNK = 256
VMEM_LIMIT = 48 * 1024 * 1024
ATTN_SCHED_FLAGS = None


def _rms(x):
    return x * lax.rsqrt(jnp.mean(x * x, axis=-1, keepdims=True) + NORM_EPS)


def _apply_rope(x, cos, sin_lo, sin_hi, half):
    return (x * cos + pltpu.roll(x, LANES - half, 1) * sin_lo
            + pltpu.roll(x, half, 1) * sin_hi)


def _mla_proj_kernel(x_ref, pos_ref, invf_ref, g_ref, wd_ref, qg_ref, kvg_ref, wuq_ref, wuk_ref,
                     wuvt_ref, ones_ref, q_ref, k_ref, vt_ref):
    h = _rms(x_ref[...]) * g_ref[...]
    down = jnp.dot(h.astype(BF16), wd_ref[...], preferred_element_type=F32)
    c_q = down[:, :MLA_Q_LORA]
    c_kv = down[:, MLA_Q_LORA:MLA_Q_LORA + MLA_KV_LORA]
    k_rope = down[:, MLA_Q_LORA + MLA_KV_LORA:]
    cqn = (_rms(c_q) * qg_ref[...]).astype(BF16)
    ckvn = (_rms(c_kv) * kvg_ref[...]).astype(BF16)
    q = jnp.dot(cqn, wuq_ref[...], preferred_element_type=F32)
    kn = jnp.dot(ckvn, wuk_ref[...], preferred_element_type=F32)
    vt = lax.dot_general(wuvt_ref[...], ckvn, (((1,), (1,)), ((), ())), preferred_element_type=F32)
    vt_ref[...] = (vt + ones_ref[...]).astype(BF16)

    ang = pos_ref[...].astype(F32) * invf_ref[...]
    cos = jnp.cos(ang)
    sin = jnp.sin(ang)
    lane = lax.broadcasted_iota(jnp.int32, ang.shape, 1)
    half = MLA_ROPE // 2
    sin_lo = jnp.where(lane < MLA_NOPE + half, -sin, 0.0)
    sin_hi = jnp.where(lane >= MLA_NOPE + half, sin, 0.0)
    kr = _apply_rope(k_rope, cos, sin_lo, sin_hi, half)
    qs = (MLA_QK ** -0.5) * LOG2E
    cos_q, sin_lo_q, sin_hi_q = cos * qs, sin_lo * qs, sin_hi * qs
    for hd in range(MLA_HEADS):
        blk = slice(hd * LANES, (hd + 1) * LANES)
        q_ref[:, blk] = _apply_rope(q[:, blk], cos_q, sin_lo_q, sin_hi_q, half).astype(BF16)
        k_ref[:, blk] = (kn[:, blk] + kr).astype(BF16)


def _mla_proj(x2, pos2, invf, g, wd, qg, kvg, wuq, wuk, wuvt, ones_col):
    t = x2.shape[0]
    tm = TOKEN_TILE
    full = lambda a: pl.BlockSpec(a.shape, lambda i: (0, 0))
    vt_rows = wuvt.shape[0]
    return pl.pallas_call(
        _mla_proj_kernel,
        grid=(t // tm,),
        in_specs=[pl.BlockSpec((tm, D_MODEL), lambda i: (i, 0)),
                  pl.BlockSpec((tm, 1), lambda i: (i, 0)),
                  full(invf), full(g), full(wd), full(qg), full(kvg), full(wuq), full(wuk),
                  full(wuvt), full(ones_col)],
        out_specs=[pl.BlockSpec((tm, MLA_HEADS * LANES), lambda i: (i, 0)),
                   pl.BlockSpec((tm, MLA_HEADS * LANES), lambda i: (i, 0)),
                   pl.BlockSpec((vt_rows, tm), lambda i: (0, i))],
        out_shape=[jax.ShapeDtypeStruct((t, MLA_HEADS * LANES), BF16),
                   jax.ShapeDtypeStruct((t, MLA_HEADS * LANES), BF16),
                   jax.ShapeDtypeStruct((vt_rows, t), BF16)],
        compiler_params=pltpu.CompilerParams(dimension_semantics=("parallel",),
                                             vmem_limit_bytes=VMEM_LIMIT),
        name="mla_proj",
    )(x2, pos2, invf, g, wd, qg, kvg, wuq, wuk, wuvt, ones_col)


def _attention_units(units, kq_fn, vt_fn, st_ref, n_val, seq, finish_fn):
    kc = KEY_CHUNK
    nchunk = seq // kc

    def scores_chunk(ui, c, m):
        k, q = kq_fn(units[ui], c)
        st = lax.dot_general(k, q, (((1,), (1,)), ((), ())), preferred_element_type=F32)
        st_ref[ui % 2, c * kc:(c + 1) * kc, :] = st
        mc = jnp.max(st, axis=0, keepdims=True)
        return mc if m is None else jnp.maximum(m, mc)

    m_next = None
    for c in range(nchunk):
        m_next = scores_chunk(0, c, m_next)
    for ui in range(len(units)):
        m_cur, m_next, r, pt_prev = m_next, None, None, None
        for c in range(nchunk + 1):
            if ui + 1 < len(units) and c < nchunk:
                m_next = scores_chunk(ui + 1, c, m_next)
            if pt_prev is not None:
                rc = jnp.dot(vt_fn(units[ui], c - 1), pt_prev, preferred_element_type=F32)
                r = rc if r is None else r + rc
            if c < nchunk:
                pt_prev = jnp.exp2(st_ref[ui % 2, c * kc:(c + 1) * kc, :] - m_cur).astype(BF16)
        finish_fn(units[ui], r[:n_val] / r[n_val:n_val + 1])


def _mla_attn_kernel(q_ref, k_ref, vt_ref, o_ref, st_ref):
    seq = q_ref.shape[0]
    tq = st_ref.shape[2]
    rows = MLA_V + VT_ONES_ROWS
    kc = KEY_CHUNK
    units = [(hh, qi) for qi in range(seq // tq) for hh in range(2)]
    done = {}

    def kq_fn(unit, c):
        hh, qi = unit
        blk = slice(hh * LANES, (hh + 1) * LANES)
        return k_ref[c * kc:(c + 1) * kc, blk], q_ref[qi * tq:(qi + 1) * tq, blk]

    def vt_fn(unit, c):
        hh, _ = unit
        return vt_ref[hh * rows:(hh + 1) * rows, c * kc:(c + 1) * kc]

    def finish_fn(unit, ot):
        hh, qi = unit
        done[hh] = ot
        if hh == 1:
            o_ref[qi * tq:(qi + 1) * tq, :] = jnp.concatenate([done[0], done[1]],
                                                              axis=0).T.astype(BF16)

    _attention_units(units, kq_fn, vt_fn, st_ref, MLA_V, seq, finish_fn)


def _mla_attn(q, k, vt, batch, seq):
    t = q.shape[0]
    rows = 2 * (MLA_V + VT_ONES_ROWS)
    return pl.pallas_call(
        _mla_attn_kernel,
        grid=(batch, MLA_HEADS // 2),
        in_specs=[pl.BlockSpec((seq, 2 * LANES), lambda b, h: (b, h)),
                  pl.BlockSpec((seq, 2 * LANES), lambda b, h: (b, h)),
                  pl.BlockSpec((rows, seq), lambda b, h: (h, b))],
        out_specs=pl.BlockSpec((seq, 2 * MLA_V), lambda b, h: (b, h)),
        out_shape=jax.ShapeDtypeStruct((t, MLA_HEADS * MLA_V), BF16),
        scratch_shapes=[pltpu.VMEM((2, seq, Q_TILE), F32)],
        compiler_params=pltpu.CompilerParams(
            dimension_semantics=("parallel", "parallel"),
            vmem_limit_bytes=VMEM_LIMIT, flags=ATTN_SCHED_FLAGS),
        name="mla_attn",
    )(q, k, vt)


def _diff_proj_kernel(x_ref, pos_ref, invf_ref, g_ref, w_ref, wvt_ref, ones_ref, q_ref, k_ref,
                      vt_ref):
    h = (_rms(x_ref[...]) * g_ref[...]).astype(BF16)
    qkv = jnp.dot(h, w_ref[...], preferred_element_type=F32)
    vt = lax.dot_general(wvt_ref[...], h, (((1,), (1,)), ((), ())), preferred_element_type=F32)
    vt_ref[...] = (vt + ones_ref[...]).astype(BF16)

    ang = pos_ref[...].astype(F32) * invf_ref[...]
    cos = jnp.cos(ang)
    sin = jnp.sin(ang)
    lane = lax.broadcasted_iota(jnp.int32, ang.shape, 1)
    half = DIFF_ROPE // 2
    sin_lo = jnp.where(lane % DIFF_HEAD_DIM < half, -sin, 0.0)
    sin_hi = jnp.where(lane % DIFF_HEAD_DIM >= half, sin, 0.0)
    qs = (DIFF_HEAD_DIM ** -0.5) * LOG2E
    cos_q, sin_lo_q, sin_hi_q = cos * qs, sin_lo * qs, sin_hi * qs
    for hd in range(DIFF_HEADS):
        blk = slice(hd * LANES, (hd + 1) * LANES)
        kblk = slice(DIFF_QK_WIDTH + hd * LANES, DIFF_QK_WIDTH + (hd + 1) * LANES)
        q_ref[:, blk] = _apply_rope(qkv[:, blk], cos_q, sin_lo_q, sin_hi_q, half).astype(BF16)
        k_ref[:, blk] = _apply_rope(qkv[:, kblk], cos, sin_lo, sin_hi, half).astype(BF16)


def _diff_proj(x2, pos2, invf, g, w, wvt, ones_col):
    t = x2.shape[0]
    tm = TOKEN_TILE
    vt_rows = wvt.shape[0]
    full = lambda a: pl.BlockSpec(a.shape, lambda i: (0, 0))
    row = lambda n: pl.BlockSpec((tm, n), lambda i: (i, 0))
    return pl.pallas_call(
        _diff_proj_kernel,
        grid=(t // tm,),
        in_specs=[row(D_MODEL), row(1), full(invf), full(g), full(w), full(wvt), full(ones_col)],
        out_specs=[row(DIFF_QK_WIDTH), row(DIFF_QK_WIDTH),
                   pl.BlockSpec((vt_rows, tm), lambda i: (0, i))],
        out_shape=[jax.ShapeDtypeStruct((t, DIFF_QK_WIDTH), BF16),
                   jax.ShapeDtypeStruct((t, DIFF_QK_WIDTH), BF16),
                   jax.ShapeDtypeStruct((vt_rows, t), BF16)],
        compiler_params=pltpu.CompilerParams(dimension_semantics=("parallel",),
                                             vmem_limit_bytes=VMEM_LIMIT),
        name="diff_proj",
    )(x2, pos2, invf, g, w, wvt, ones_col)


def _diff_attn_kernel(lam_ref, q_ref, k_ref, vt_ref, o_ref, st_ref, *, lam_init):
    seq = q_ref.shape[0]
    tq = st_ref.shape[2]
    kc = KEY_CHUNK
    lam_p = lam_ref[...]
    lam = (jnp.exp(jnp.sum(lam_p[0:1] * lam_p[1:2], axis=-1, keepdims=True))
           - jnp.exp(jnp.sum(lam_p[2:3] * lam_p[3:4], axis=-1, keepdims=True)) + lam_init)
    lane = lax.broadcasted_iota(jnp.int32, (tq, LANES), 1)
    units = [(comp, qi) for qi in range(seq // tq) for comp in range(2)]
    masked_q = {}
    done = {}

    def kq_fn(unit, c):
        comp, qi = unit
        if unit not in masked_q:
            q = q_ref[qi * tq:(qi + 1) * tq, :]
            in_comp = (lane < DIFF_HEAD_DIM) if comp == 0 else (lane >= DIFF_HEAD_DIM)
            masked_q[unit] = jnp.where(in_comp, q, jnp.zeros_like(q))
        return k_ref[c * kc:(c + 1) * kc, :], masked_q[unit]

    def vt_fn(unit, c):
        return vt_ref[:, c * kc:(c + 1) * kc]

    def finish_fn(unit, ot):
        comp, qi = unit
        done[comp] = ot
        if comp == 1:
            o = done[0] - lam * done[1]
            o = o * lax.rsqrt(jnp.mean(o * o, axis=0, keepdims=True) + NORM_EPS) * (1.0 - lam_init)
            o_ref[qi * tq:(qi + 1) * tq, :] = o.T.astype(BF16)

    _attention_units(units, kq_fn, vt_fn, st_ref, DIFF_V, seq, finish_fn)


def _diff_attn(lam_params, q, k, vt, batch, seq, lam_init):
    t = q.shape[0]
    return pl.pallas_call(
        functools.partial(_diff_attn_kernel, lam_init=lam_init),
        grid=(batch, DIFF_HEADS),
        in_specs=[pl.BlockSpec(lam_params.shape, lambda b, h: (0, 0)),
                  pl.BlockSpec((seq, LANES), lambda b, h: (b, h)),
                  pl.BlockSpec((seq, LANES), lambda b, h: (b, h)),
                  pl.BlockSpec((DIFF_V + VT_ONES_ROWS, seq), lambda b, h: (h, b))],
        out_specs=pl.BlockSpec((seq, LANES), lambda b, h: (b, h)),
        out_shape=jax.ShapeDtypeStruct((t, DIFF_V_WIDTH), BF16),
        scratch_shapes=[pltpu.VMEM((2, seq, Q_TILE), F32)],
        compiler_params=pltpu.CompilerParams(
            dimension_semantics=("parallel", "parallel"),
            vmem_limit_bytes=VMEM_LIMIT, flags=ATTN_SCHED_FLAGS),
        name="diff_attn",
    )(lam_params, q, k, vt)


def _out_proj_kernel(o_ref, x_ref, w_ref, g_ref, xn_ref, h_ref):
    xn = x_ref[...] + jnp.dot(o_ref[...], w_ref[...], preferred_element_type=F32)
    xn_ref[...] = xn
    h_ref[...] = (_rms(xn) * g_ref[...]).astype(BF16)


def _out_proj(o, x2, w, g):
    t = x2.shape[0]
    tm = TOKEN_TILE
    full = lambda a: pl.BlockSpec(a.shape, lambda i: (0, 0))
    row = lambda n: pl.BlockSpec((tm, n), lambda i: (i, 0))
    return pl.pallas_call(
        _out_proj_kernel,
        grid=(t // tm,),
        in_specs=[row(o.shape[1]), row(D_MODEL), full(w), full(g)],
        out_specs=[row(D_MODEL), row(D_MODEL)],
        out_shape=[jax.ShapeDtypeStruct((t, D_MODEL), F32),
                   jax.ShapeDtypeStruct((t, D_MODEL), BF16)],
        compiler_params=pltpu.CompilerParams(dimension_semantics=("parallel",),
                                             vmem_limit_bytes=VMEM_LIMIT),
        name="out_proj",
    )(o, x2, w, g)


def _ffn_kernel(x_ref, h_ref, hprev_ref, hnext_ref, wup_ref, cw_ref, cb_ref, wdn_ref, fg_ref,
                out_ref, hext_ref, u_ref, acc_ref, *, tiles_per_seq, apply_final_norm):
    i = pl.program_id(0)
    f = pl.program_id(1)
    tm = x_ref.shape[0]
    hb = BF16_SUBLANES
    fc = wdn_ref.shape[0]

    @pl.when(f == 0)
    def _():
        at_start = (i % tiles_per_seq) == 0
        at_end = (i % tiles_per_seq) == tiles_per_seq - 1
        hprev = hprev_ref[...]
        hnext = hnext_ref[...]
        hext_ref[0:hb, :] = jnp.where(at_start, jnp.zeros_like(hprev), hprev)
        hext_ref[hb:hb + tm, :] = h_ref[...]
        hext_ref[hb + tm:, :] = jnp.where(at_end, jnp.zeros_like(hnext), hnext)
        acc_ref[...] = x_ref[...]

    u_ref[...] = jnp.dot(hext_ref[...], wup_ref[...], preferred_element_type=F32)
    cw = cw_ref[...]
    uc = (u_ref[hb - 1:hb - 1 + tm, :] * cw[0:1] + u_ref[hb:hb + tm, :] * cw[1:2]
          + u_ref[hb + 1:hb + 1 + tm, :] * cw[2:3] + cb_ref[...])
    gate = uc[:, :fc]
    val = uc[:, fc:]
    act = (gate * jax.nn.sigmoid(gate) * val).astype(BF16)
    acc_ref[...] += jnp.dot(act, wdn_ref[...], preferred_element_type=F32)

    @pl.when(f == pl.num_programs(1) - 1)
    def _():
        y = acc_ref[...]
        if apply_final_norm:
            y = _rms(y) * fg_ref[...]
        out_ref[...] = y


def _ffn(xn, h, wup, cw, cb, wdn, fg, seq, apply_final_norm):
    t = xn.shape[0]
    tm = TOKEN_TILE
    fc = FF_CHUNK
    hb = BF16_SUBLANES
    nf = D_FF // fc
    halo_per_tile = tm // hb
    last_halo = t // hb - 1
    return pl.pallas_call(
        functools.partial(_ffn_kernel, tiles_per_seq=seq // tm, apply_final_norm=apply_final_norm),
        grid=(t // tm, nf),
        in_specs=[pl.BlockSpec((tm, D_MODEL), lambda i, f: (i, 0)),
                  pl.BlockSpec((tm, D_MODEL), lambda i, f: (i, 0)),
                  pl.BlockSpec((hb, D_MODEL), lambda i, f: (jnp.maximum(i * halo_per_tile - 1, 0), 0)),
                  pl.BlockSpec((hb, D_MODEL),
                               lambda i, f: (jnp.minimum((i + 1) * halo_per_tile, last_halo), 0)),
                  pl.BlockSpec((D_MODEL, 2 * fc), lambda i, f: (0, f)),
                  pl.BlockSpec((CONV_WIDTH, 2 * fc), lambda i, f: (0, f)),
                  pl.BlockSpec((1, 2 * fc), lambda i, f: (0, f)),
                  pl.BlockSpec((fc, D_MODEL), lambda i, f: (f, 0)),
                  pl.BlockSpec((1, D_MODEL), lambda i, f: (0, 0))],
        out_specs=pl.BlockSpec((tm, D_MODEL), lambda i, f: (i, 0)),
        out_shape=jax.ShapeDtypeStruct((t, D_MODEL), F32),
        scratch_shapes=[pltpu.VMEM((tm + 2 * hb, D_MODEL), BF16),
                        pltpu.VMEM((tm + 2 * hb, 2 * fc), F32),
                        pltpu.VMEM((tm, D_MODEL), F32)],
        compiler_params=pltpu.CompilerParams(dimension_semantics=("parallel", "arbitrary"),
                                             vmem_limit_bytes=VMEM_LIMIT),
        name="conv_ffn",
    )(xn, h, h, h, wup, cw, cb, wdn, fg)


def _inv_freq_lanes(rot_dim, period, offset):
    inv = ROPE_THETA ** (-jnp.arange(0, rot_dim, 2, dtype=F32) / rot_dim)
    one = jnp.zeros((period,), F32).at[offset:offset + rot_dim].set(jnp.concatenate([inv, inv]))
    return jnp.tile(one, LANES // period)[None, :]


def _pad_heads(w, heads, width):
    w = w.reshape(w.shape[0], heads, width)
    return jnp.pad(w, ((0, 0), (0, 0), (0, LANES - width))).reshape(w.shape[0], heads * LANES)


def _value_weight_t(w, heads, width):
    wt = w.T.reshape(heads, width, w.shape[0])
    wt = jnp.pad(wt, ((0, 0), (0, VT_ONES_ROWS), (0, 0))).reshape(-1, w.shape[0])
    ones = jnp.pad(jnp.zeros((heads, width, 1), F32), ((0, 0), (0, VT_ONES_ROWS), (0, 0)),
                   constant_values=1.0).reshape(-1, 1)
    return wt.astype(BF16), ones


def _chunk_gate_val(a, fc):
    lead = a.shape[:-1]
    a = a.reshape(lead + (2, D_FF // fc, fc))
    return jnp.swapaxes(a, -3, -2).reshape(lead + (2 * D_FF,))


def kernel(x, positions, attn_norm_g, ffn_norm_g, final_norm_g, mla_w_down, mla_q_norm_g,
           mla_kv_norm_g, mla_w_uq, mla_w_ukv, mla_w_o, diff_w_qkv, diff_lam_q1, diff_lam_k1,
           diff_lam_q2, diff_lam_k2, diff_w_o, ffn_w_up, ffn_conv_w, ffn_conv_b, ffn_w_down):
    batch, seq, _ = x.shape
    t = batch * seq
    x2 = x.reshape(t, D_MODEL)
    pos2 = positions.reshape(t, 1)
    row = lambda a: a.reshape(1, -1)

    def ffn(xn, h, layer, final):
        return _ffn(xn, h,
                    _chunk_gate_val(ffn_w_up[layer], FF_CHUNK).astype(BF16),
                    _chunk_gate_val(ffn_conv_w[layer], FF_CHUNK),
                    row(_chunk_gate_val(ffn_conv_b[layer], FF_CHUNK)),
                    ffn_w_down[layer].astype(BF16), row(final_norm_g), seq, final)

    wd = mla_w_down[0]
    k_rope_cols = jnp.pad(wd[:, MLA_Q_LORA + MLA_KV_LORA:], ((0, 0), (MLA_NOPE, LANES - MLA_QK)))
    wd_p = jnp.concatenate([wd[:, :MLA_Q_LORA + MLA_KV_LORA], k_rope_cols], axis=1).astype(BF16)
    wukv = mla_w_ukv[0].reshape(MLA_KV_LORA, MLA_HEADS, MLA_NOPE + MLA_V)
    wuk_p = _pad_heads(wukv[:, :, :MLA_NOPE].reshape(MLA_KV_LORA, -1), MLA_HEADS, MLA_NOPE)
    wuvt, ones_col = _value_weight_t(wukv[:, :, MLA_NOPE:].reshape(MLA_KV_LORA, -1), MLA_HEADS,
                                     MLA_V)
    q, k, vt = _mla_proj(x2, pos2, _inv_freq_lanes(MLA_ROPE, LANES, MLA_NOPE), row(attn_norm_g[0]),
                         wd_p, row(mla_q_norm_g[0]), row(mla_kv_norm_g[0]),
                         _pad_heads(mla_w_uq[0], MLA_HEADS, MLA_QK).astype(BF16),
                         wuk_p.astype(BF16), wuvt, ones_col)
    o = _mla_attn(q, k, vt, batch, seq)
    xn, h = _out_proj(o, x2, mla_w_o[0].astype(BF16), row(ffn_norm_g[0]))
    x2 = ffn(xn, h, 0, False)

    lam_init = 0.8 - 0.6 * math.exp(-0.3 * 1)
    wvt, ones_col = _value_weight_t(diff_w_qkv[0][:, 2 * DIFF_QK_WIDTH:], DIFF_HEADS, DIFF_V)
    q, k, vt = _diff_proj(x2, pos2, _inv_freq_lanes(DIFF_ROPE, DIFF_HEAD_DIM, 0),
                          row(attn_norm_g[1]), diff_w_qkv[0][:, :2 * DIFF_QK_WIDTH].astype(BF16),
                          wvt, ones_col)
    lam_params = jnp.stack([diff_lam_q1[0], diff_lam_k1[0], diff_lam_q2[0], diff_lam_k2[0]])
    o = _diff_attn(lam_params, q, k, vt, batch, seq, lam_init)
    xn, h = _out_proj(o, x2, diff_w_o[0].astype(BF16), row(ffn_norm_g[1]))
    x2 = ffn(xn, h, 1, True)
    return x2.reshape(batch, seq, D_MODEL)
```

```python
import functools
import math

import jax
import jax.numpy as jnp
from jax import lax
from jax.experimental import pallas as pl
from jax.experimental.pallas import tpu as pltpu

F32 = jnp.float32
BF16 = jnp.bfloat16

D_MODEL = 1024
ROPE_THETA = 500000.0
NORM_EPS = 1e-6

MLA_HEADS = 16
MLA_Q_LORA = 384
MLA_KV_LORA = 256
MLA_NOPE = 64
MLA_ROPE = 32
MLA_V = 64
MLA_QK = MLA_NOPE + MLA_ROPE

DIFF_HEADS = 8
DIFF_HEAD_DIM = 64
DIFF_V = 2 * DIFF_HEAD_DIM
DIFF_ROPE = DIFF_HEAD_DIM // 4
DIFF_QK_WIDTH = DIFF_HEADS * 2 * DIFF_HEAD_DIM
DIFF_V_WIDTH = DIFF_HEADS * DIFF_V

D_FF = 2816
CONV_WIDTH = 3

LANES = 128
BF16_SUBLANES = 16
VT_ONES_ROWS = BF16_SUBLANES
LOG2E = math.log2(math.e)

TOKEN_TILE = 512
Q_TILE = 512
KEY_CHUNK = 256
FF_CHUNK = 256
VMEM_LIMIT = 48 * 1024 * 1024
ATTN_SCHED_FLAGS = None


def _rms(x):
    return x * lax.rsqrt(jnp.mean(x * x, axis=-1, keepdims=True) + NORM_EPS)


def _apply_rope(x, cos, sin_lo, sin_hi, half):
    return (x * cos + pltpu.roll(x, LANES - half, 1) * sin_lo
            + pltpu.roll(x, half, 1) * sin_hi)


def _mla_proj_kernel(x_ref, pos_ref, invf_ref, g_ref, wd_ref, qg_ref, kvg_ref, wuq_ref, wuk_ref,
                     wuvt_ref, ones_ref, q_ref, k_ref, vt_ref):
    h = _rms(x_ref[...]) * g_ref[...]
    down = jnp.dot(h.astype(BF16), wd_ref[...], preferred_element_type=F32)
    c_q = down[:, :MLA_Q_LORA]
    c_kv = down[:, MLA_Q_LORA:MLA_Q_LORA + MLA_KV_LORA]
    k_rope = down[:, MLA_Q_LORA + MLA_KV_LORA:]
    cqn = (_rms(c_q) * qg_ref[...]).astype(BF16)
    ckvn = (_rms(c_kv) * kvg_ref[...]).astype(BF16)
    q = jnp.dot(cqn, wuq_ref[...], preferred_element_type=F32)
    kn = jnp.dot(ckvn, wuk_ref[...], preferred_element_type=F32)
    vt = lax.dot_general(wuvt_ref[...], ckvn, (((1,), (1,)), ((), ())), preferred_element_type=F32)
    vt_ref[...] = (vt + ones_ref[...]).astype(BF16)

    ang = pos_ref[...].astype(F32) * invf_ref[...]
    cos = jnp.cos(ang)
    sin = jnp.sin(ang)
    lane = lax.broadcasted_iota(jnp.int32, ang.shape, 1)
    half = MLA_ROPE // 2
    sin_lo = jnp.where(lane < MLA_NOPE + half, -sin, 0.0)
    sin_hi = jnp.where(lane >= MLA_NOPE + half, sin, 0.0)
    kr = _apply_rope(k_rope, cos, sin_lo, sin_hi, half)
    qs = (MLA_QK ** -0.5) * LOG2E
    cos_q, sin_lo_q, sin_hi_q = cos * qs, sin_lo * qs, sin_hi * qs
    for hd in range(MLA_HEADS):
        blk = slice(hd * LANES, (hd + 1) * LANES)
        q_ref[:, blk] = _apply_rope(q[:, blk], cos_q, sin_lo_q, sin_hi_q, half).astype(BF16)
        k_ref[:, blk] = (kn[:, blk] + kr).astype(BF16)


def _mla_proj(x2, pos2, invf, g, wd, qg, kvg, wuq, wuk, wuvt, ones_col):
    t = x2.shape[0]
    tm = TOKEN_TILE
    full = lambda a: pl.BlockSpec(a.shape, lambda i: (0, 0))
    vt_rows = wuvt.shape[0]
    return pl.pallas_call(
        _mla_proj_kernel,
        grid=(t // tm,),
        in_specs=[pl.BlockSpec((tm, D_MODEL), lambda i: (i, 0)),
                  pl.BlockSpec((tm, 1), lambda i: (i, 0)),
                  full(invf), full(g), full(wd), full(qg), full(kvg), full(wuq), full(wuk),
                  full(wuvt), full(ones_col)],
        out_specs=[pl.BlockSpec((tm, MLA_HEADS * LANES), lambda i: (i, 0)),
                   pl.BlockSpec((tm, MLA_HEADS * LANES), lambda i: (i, 0)),
                   pl.BlockSpec((vt_rows, tm), lambda i: (0, i))],
        out_shape=[jax.ShapeDtypeStruct((t, MLA_HEADS * LANES), BF16),
                   jax.ShapeDtypeStruct((t, MLA_HEADS * LANES), BF16),
                   jax.ShapeDtypeStruct((vt_rows, t), BF16)],
        compiler_params=pltpu.CompilerParams(dimension_semantics=("parallel",),
                                             vmem_limit_bytes=VMEM_LIMIT),
        name="mla_proj",
    )(x2, pos2, invf, g, wd, qg, kvg, wuq, wuk, wuvt, ones_col)


def _attention_units(units, kq_fn, vt_fn, st_ref, n_val, seq, finish_fn):
    kc = KEY_CHUNK
    nchunk = seq // kc

    def scores_chunk(ui, c, m):
        k, q = kq_fn(units[ui], c)
        st = lax.dot_general(k, q, (((1,), (1,)), ((), ())), preferred_element_type=F32)
        st_ref[ui % 2, c * kc:(c + 1) * kc, :] = st
        mc = jnp.max(st, axis=0, keepdims=True)
        return mc if m is None else jnp.maximum(m, mc)

    m_next = None
    for c in range(nchunk):
        m_next = scores_chunk(0, c, m_next)
    for ui in range(len(units)):
        m_cur, m_next, r, pt_prev = m_next, None, None, None
        for c in range(nchunk + 1):
            if ui + 1 < len(units) and c < nchunk:
                m_next = scores_chunk(ui + 1, c, m_next)
            if pt_prev is not None:
                rc = jnp.dot(vt_fn(units[ui], c - 1), pt_prev, preferred_element_type=F32)
                r = rc if r is None else r + rc
            if c < nchunk:
                pt_prev = jnp.exp2(st_ref[ui % 2, c * kc:(c + 1) * kc, :] - m_cur).astype(BF16)
        finish_fn(units[ui], r[:n_val] / r[n_val:n_val + 1])


def _mla_attn_kernel(q_ref, k_ref, vt_ref, o_ref, st_ref):
    seq = q_ref.shape[0]
    tq = st_ref.shape[2]
    rows = MLA_V + VT_ONES_ROWS
    kc = KEY_CHUNK
    units = [(hh, qi) for qi in range(seq // tq) for hh in range(2)]
    done = {}

    def kq_fn(unit, c):
        hh, qi = unit
        blk = slice(hh * LANES, (hh + 1) * LANES)
        return k_ref[c * kc:(c + 1) * kc, blk], q_ref[qi * tq:(qi + 1) * tq, blk]

    def vt_fn(unit, c):
        hh, _ = unit
        return vt_ref[hh * rows:(hh + 1) * rows, c * kc:(c + 1) * kc]

    def finish_fn(unit, ot):
        hh, qi = unit
        done[hh] = ot
        if hh == 1:
            o_ref[qi * tq:(qi + 1) * tq, :] = jnp.concatenate([done[0], done[1]],
                                                              axis=0).T.astype(BF16)

    _attention_units(units, kq_fn, vt_fn, st_ref, MLA_V, seq, finish_fn)


def _mla_attn(q, k, vt, batch, seq):
    t = q.shape[0]
    rows = 2 * (MLA_V + VT_ONES_ROWS)
    return pl.pallas_call(
        _mla_attn_kernel,
        grid=(batch, MLA_HEADS // 2),
        in_specs=[pl.BlockSpec((seq, 2 * LANES), lambda b, h: (b, h)),
                  pl.BlockSpec((seq, 2 * LANES), lambda b, h: (b, h)),
                  pl.BlockSpec((rows, seq), lambda b, h: (h, b))],
        out_specs=pl.BlockSpec((seq, 2 * MLA_V), lambda b, h: (b, h)),
        out_shape=jax.ShapeDtypeStruct((t, MLA_HEADS * MLA_V), BF16),
        scratch_shapes=[pltpu.VMEM((2, seq, Q_TILE), F32)],
        compiler_params=pltpu.CompilerParams(
            dimension_semantics=("parallel", "parallel"),
            vmem_limit_bytes=VMEM_LIMIT, flags=ATTN_SCHED_FLAGS),
        name="mla_attn",
    )(q, k, vt)


def _diff_proj_kernel(x_ref, pos_ref, invf_ref, g_ref, w_ref, wvt_ref, ones_ref, q_ref, k_ref,
                      vt_ref):
    h = (_rms(x_ref[...]) * g_ref[...]).astype(BF16)
    qkv = jnp.dot(h, w_ref[...], preferred_element_type=F32)
    vt = lax.dot_general(wvt_ref[...], h, (((1,), (1,)), ((), ())), preferred_element_type=F32)
    vt_ref[...] = (vt + ones_ref[...]).astype(BF16)

    ang = pos_ref[...].astype(F32) * invf_ref[...]
    cos = jnp.cos(ang)
    sin = jnp.sin(ang)
    lane = lax.broadcasted_iota(jnp.int32, ang.shape, 1)
    half = DIFF_ROPE // 2
    sin_lo = jnp.where(lane % DIFF_HEAD_DIM < half, -sin, 0.0)
    sin_hi = jnp.where(lane % DIFF_HEAD_DIM >= half, sin, 0.0)
    qs = (DIFF_HEAD_DIM ** -0.5) * LOG2E
    cos_q, sin_lo_q, sin_hi_q = cos * qs, sin_lo * qs, sin_hi * qs
    for hd in range(DIFF_HEADS):
        blk = slice(hd * LANES, (hd + 1) * LANES)
        kblk = slice(DIFF_QK_WIDTH + hd * LANES, DIFF_QK_WIDTH + (hd + 1) * LANES)
        q_ref[:, blk] = _apply_rope(qkv[:, blk], cos_q, sin_lo_q, sin_hi_q, half).astype(BF16)
        k_ref[:, blk] = _apply_rope(qkv[:, kblk], cos, sin_lo, sin_hi, half).astype(BF16)


def _diff_proj(x2, pos2, invf, g, w, wvt, ones_col):
    t = x2.shape[0]
    tm = TOKEN_TILE
    vt_rows = wvt.shape[0]
    full = lambda a: pl.BlockSpec(a.shape, lambda i: (0, 0))
    row = lambda n: pl.BlockSpec((tm, n), lambda i: (i, 0))
    return pl.pallas_call(
        _diff_proj_kernel,
        grid=(t // tm,),
        in_specs=[row(D_MODEL), row(1), full(invf), full(g), full(w), full(wvt), full(ones_col)],
        out_specs=[row(DIFF_QK_WIDTH), row(DIFF_QK_WIDTH),
                   pl.BlockSpec((vt_rows, tm), lambda i: (0, i))],
        out_shape=[jax.ShapeDtypeStruct((t, DIFF_QK_WIDTH), BF16),
                   jax.ShapeDtypeStruct((t, DIFF_QK_WIDTH), BF16),
                   jax.ShapeDtypeStruct((vt_rows, t), BF16)],
        compiler_params=pltpu.CompilerParams(dimension_semantics=("parallel",),
                                             vmem_limit_bytes=VMEM_LIMIT),
        name="diff_proj",
    )(x2, pos2, invf, g, w, wvt, ones_col)


def _diff_attn_kernel(lam_ref, q_ref, k_ref, vt_ref, o_ref, st_ref, *, lam_init):
    seq = q_ref.shape[0]
    tq = st_ref.shape[2]
    kc = KEY_CHUNK
    lam_p = lam_ref[...]
    lam = (jnp.exp(jnp.sum(lam_p[0:1] * lam_p[1:2], axis=-1, keepdims=True))
           - jnp.exp(jnp.sum(lam_p[2:3] * lam_p[3:4], axis=-1, keepdims=True)) + lam_init)
    lane = lax.broadcasted_iota(jnp.int32, (tq, LANES), 1)
    units = [(comp, qi) for qi in range(seq // tq) for comp in range(2)]
    masked_q = {}
    done = {}

    def kq_fn(unit, c):
        comp, qi = unit
        if unit not in masked_q:
            q = q_ref[qi * tq:(qi + 1) * tq, :]
            in_comp = (lane < DIFF_HEAD_DIM) if comp == 0 else (lane >= DIFF_HEAD_DIM)
            masked_q[unit] = jnp.where(in_comp, q, jnp.zeros_like(q))
        return k_ref[c * kc:(c + 1) * kc, :], masked_q[unit]

    def vt_fn(unit, c):
        return vt_ref[:, c * kc:(c + 1) * kc]

    def finish_fn(unit, ot):
        comp, qi = unit
        done[comp] = ot
        if comp == 1:
            o = done[0] - lam * done[1]
            o = o * lax.rsqrt(jnp.mean(o * o, axis=0, keepdims=True) + NORM_EPS) * (1.0 - lam_init)
            o_ref[qi * tq:(qi + 1) * tq, :] = o.T.astype(BF16)

    _attention_units(units, kq_fn, vt_fn, st_ref, DIFF_V, seq, finish_fn)


def _diff_attn(lam_params, q, k, vt, batch, seq, lam_init):
    t = q.shape[0]
    return pl.pallas_call(
        functools.partial(_diff_attn_kernel, lam_init=lam_init),
        grid=(batch, DIFF_HEADS),
        in_specs=[pl.BlockSpec(lam_params.shape, lambda b, h: (0, 0)),
                  pl.BlockSpec((seq, LANES), lambda b, h: (b, h)),
                  pl.BlockSpec((seq, LANES), lambda b, h: (b, h)),
                  pl.BlockSpec((DIFF_V + VT_ONES_ROWS, seq), lambda b, h: (h, b))],
        out_specs=pl.BlockSpec((seq, LANES), lambda b, h: (b, h)),
        out_shape=jax.ShapeDtypeStruct((t, DIFF_V_WIDTH), BF16),
        scratch_shapes=[pltpu.VMEM((2, seq, Q_TILE), F32)],
        compiler_params=pltpu.CompilerParams(
            dimension_semantics=("parallel", "parallel"),
            vmem_limit_bytes=VMEM_LIMIT, flags=ATTN_SCHED_FLAGS),
        name="diff_attn",
    )(lam_params, q, k, vt)


def _out_proj_kernel(o_ref, x_ref, w_ref, g_ref, xn_ref, h_ref):
    xn = x_ref[...] + jnp.dot(o_ref[...], w_ref[...], preferred_element_type=F32)
    xn_ref[...] = xn
    h_ref[...] = (_rms(xn) * g_ref[...]).astype(BF16)


def _out_proj(o, x2, w, g):
    t = x2.shape[0]
    tm = TOKEN_TILE
    full = lambda a: pl.BlockSpec(a.shape, lambda i: (0, 0))
    row = lambda n: pl.BlockSpec((tm, n), lambda i: (i, 0))
    return pl.pallas_call(
        _out_proj_kernel,
        grid=(t // tm,),
        in_specs=[row(o.shape[1]), row(D_MODEL), full(w), full(g)],
        out_specs=[row(D_MODEL), row(D_MODEL)],
        out_shape=[jax.ShapeDtypeStruct((t, D_MODEL), F32),
                   jax.ShapeDtypeStruct((t, D_MODEL), BF16)],
        compiler_params=pltpu.CompilerParams(dimension_semantics=("parallel",),
                                             vmem_limit_bytes=VMEM_LIMIT),
        name="out_proj",
    )(o, x2, w, g)


def _ffn_kernel(x_ref, h_ref, hprev_ref, hnext_ref, wg_ref, wv_ref, cwg_ref, cwv_ref, cbg_ref,
                cbv_ref, wdn_ref, fg_ref, out_ref, hext_ref, u_ref, acc_ref, *, nf, tiles_per_seq,
                apply_final_norm):
    s = pl.program_id(0)
    n_pairs = pl.num_programs(0) - 1
    up_pair = jnp.minimum(s, n_pairs - 1)
    up_tile, up_chunk = up_pair // nf, up_pair % nf
    chunk = jnp.maximum(s - 1, 0) % nf
    tm = x_ref.shape[0]
    hb = BF16_SUBLANES

    @pl.when(s == 0)
    def _():
        u_ref[1] = jnp.zeros(u_ref.shape[1:], F32)
        acc_ref[...] = jnp.zeros_like(acc_ref)

    @pl.when((up_chunk == 0) & (s < n_pairs))
    def _():
        at_start = (up_tile % tiles_per_seq) == 0
        at_end = (up_tile % tiles_per_seq) == tiles_per_seq - 1
        hprev = hprev_ref[...]
        hnext = hnext_ref[...]
        hext_ref[0:hb, :] = jnp.where(at_start, jnp.zeros_like(hprev), hprev)
        hext_ref[hb:hb + tm, :] = h_ref[...]
        hext_ref[hb + tm:, :] = jnp.where(at_end, jnp.zeros_like(hnext), hnext)

    @pl.when((s > 0) & (chunk == 0))
    def _():
        acc_ref[...] = x_ref[...]

    prev = (s + 1) % 2

    def conv(j, cw_ref, cb_ref):
        cols = slice((j % 2) * LANES, (j % 2 + 1) * LANES)
        cw = cw_ref[...]
        return (u_ref[prev, j, hb - 1:hb - 1 + tm, :] * cw[0:1, cols]
                + u_ref[prev, j, hb:hb + tm, :] * cw[1:2, cols]
                + u_ref[prev, j, hb + 1:hb + 1 + tm, :] * cw[2:3, cols] + cb_ref[:, cols])

    gate = jnp.concatenate([conv(0, cwg_ref, cbg_ref), conv(1, cwg_ref, cbg_ref)], axis=1)
    val = jnp.concatenate([conv(2, cwv_ref, cbv_ref), conv(3, cwv_ref, cbv_ref)], axis=1)
    act = (gate * jax.nn.sigmoid(gate) * val).astype(BF16)
    acc_ref[...] += jnp.dot(act, wdn_ref[...], preferred_element_type=F32)

    hx = hext_ref[...]
    ug = jnp.dot(hx, wg_ref[...], preferred_element_type=F32)
    uv = jnp.dot(hx, wv_ref[...], preferred_element_type=F32)
    for j in range(2):
        u_ref[s % 2, j] = ug[:, j * LANES:(j + 1) * LANES]
        u_ref[s % 2, 2 + j] = uv[:, j * LANES:(j + 1) * LANES]

    @pl.when((s > 0) & (chunk == nf - 1))
    def _():
        y = acc_ref[...]
        if apply_final_norm:
            y = _rms(y) * fg_ref[...]
        out_ref[...] = y


def _ffn(xn, h, wup, cw, cb, wdn, fg, seq, apply_final_norm):
    t = xn.shape[0]
    tm = TOKEN_TILE
    fc = FF_CHUNK
    hb = BF16_SUBLANES
    nf = D_FF // fc
    n_pairs = (t // tm) * nf
    halo_per_tile = tm // hb
    last_halo = t // hb - 1
    up_tile = lambda s: jnp.minimum(s, n_pairs - 1) // nf
    up_chunk = lambda s: jnp.minimum(s, n_pairs - 1) % nf
    tile = lambda s: jnp.maximum(s - 1, 0) // nf
    chunk = lambda s: jnp.maximum(s - 1, 0) % nf
    return pl.pallas_call(
        functools.partial(_ffn_kernel, nf=nf, tiles_per_seq=seq // tm,
                          apply_final_norm=apply_final_norm),
        grid=(n_pairs + 1,),
        in_specs=[pl.BlockSpec((tm, D_MODEL), lambda s: (tile(s), 0)),
                  pl.BlockSpec((tm, D_MODEL), lambda s: (up_tile(s), 0)),
                  pl.BlockSpec((hb, D_MODEL),
                               lambda s: (jnp.maximum(up_tile(s) * halo_per_tile - 1, 0), 0)),
                  pl.BlockSpec((hb, D_MODEL),
                               lambda s: (jnp.minimum((up_tile(s) + 1) * halo_per_tile, last_halo), 0)),
                  pl.BlockSpec((D_MODEL, fc), lambda s: (0, up_chunk(s))),
                  pl.BlockSpec((D_MODEL, fc), lambda s: (0, nf + up_chunk(s))),
                  pl.BlockSpec((CONV_WIDTH, fc), lambda s: (0, chunk(s))),
                  pl.BlockSpec((CONV_WIDTH, fc), lambda s: (0, nf + chunk(s))),
                  pl.BlockSpec((1, fc), lambda s: (0, chunk(s))),
                  pl.BlockSpec((1, fc), lambda s: (0, nf + chunk(s))),
                  pl.BlockSpec((fc, D_MODEL), lambda s: (chunk(s), 0)),
                  pl.BlockSpec((1, D_MODEL), lambda s: (0, 0))],
        out_specs=pl.BlockSpec((tm, D_MODEL), lambda s: (tile(s), 0)),
        out_shape=jax.ShapeDtypeStruct((t, D_MODEL), F32),
        scratch_shapes=[pltpu.VMEM((tm + 2 * hb, D_MODEL), BF16),
                        pltpu.VMEM((2, 4, tm + 2 * hb, LANES), F32),
                        pltpu.VMEM((tm, D_MODEL), F32)],
        compiler_params=pltpu.CompilerParams(dimension_semantics=("arbitrary",),
                                             vmem_limit_bytes=VMEM_LIMIT),
        name="conv_ffn",
    )(xn, h, h, h, wup, wup, cw, cw, cb, cb, wdn, fg)


def _inv_freq_lanes(rot_dim, period, offset):
    inv = ROPE_THETA ** (-jnp.arange(0, rot_dim, 2, dtype=F32) / rot_dim)
    one = jnp.zeros((period,), F32).at[offset:offset + rot_dim].set(jnp.concatenate([inv, inv]))
    return jnp.tile(one, LANES // period)[None, :]


def _pad_heads(w, heads, width):
    w = w.reshape(w.shape[0], heads, width)
    return jnp.pad(w, ((0, 0), (0, 0), (0, LANES - width))).reshape(w.shape[0], heads * LANES)


def _value_weight_t(w, heads, width):
    wt = w.T.reshape(heads, width, w.shape[0])
    wt = jnp.pad(wt, ((0, 0), (0, VT_ONES_ROWS), (0, 0))).reshape(-1, w.shape[0])
    ones = jnp.pad(jnp.zeros((heads, width, 1), F32), ((0, 0), (0, VT_ONES_ROWS), (0, 0)),
                   constant_values=1.0).reshape(-1, 1)
    return wt.astype(BF16), ones


def kernel(x, positions, attn_norm_g, ffn_norm_g, final_norm_g, mla_w_down, mla_q_norm_g,
           mla_kv_norm_g, mla_w_uq, mla_w_ukv, mla_w_o, diff_w_qkv, diff_lam_q1, diff_lam_k1,
           diff_lam_q2, diff_lam_k2, diff_w_o, ffn_w_up, ffn_conv_w, ffn_conv_b, ffn_w_down):
    batch, seq, _ = x.shape
    t = batch * seq
    x2 = x.reshape(t, D_MODEL)
    pos2 = positions.reshape(t, 1)
    row = lambda a: a.reshape(1, -1)

    def ffn(xn, h, layer, final):
        return _ffn(xn, h, ffn_w_up[layer].astype(BF16), ffn_conv_w[layer],
                    row(ffn_conv_b[layer]), ffn_w_down[layer].astype(BF16), row(final_norm_g),
                    seq, final)

    wd = mla_w_down[0]
    k_rope_cols = jnp.pad(wd[:, MLA_Q_LORA + MLA_KV_LORA:], ((0, 0), (MLA_NOPE, LANES - MLA_QK)))
    wd_p = jnp.concatenate([wd[:, :MLA_Q_LORA + MLA_KV_LORA], k_rope_cols], axis=1).astype(BF16)
    wukv = mla_w_ukv[0].reshape(MLA_KV_LORA, MLA_HEADS, MLA_NOPE + MLA_V)
    wuk_p = _pad_heads(wukv[:, :, :MLA_NOPE].reshape(MLA_KV_LORA, -1), MLA_HEADS, MLA_NOPE)
    wuvt, ones_col = _value_weight_t(wukv[:, :, MLA_NOPE:].reshape(MLA_KV_LORA, -1), MLA_HEADS,
                                     MLA_V)
    q, k, vt = _mla_proj(x2, pos2, _inv_freq_lanes(MLA_ROPE, LANES, MLA_NOPE), row(attn_norm_g[0]),
                         wd_p, row(mla_q_norm_g[0]), row(mla_kv_norm_g[0]),
                         _pad_heads(mla_w_uq[0], MLA_HEADS, MLA_QK).astype(BF16),
                         wuk_p.astype(BF16), wuvt, ones_col)
    o = _mla_attn(q, k, vt, batch, seq)
    xn, h = _out_proj(o, x2, mla_w_o[0].astype(BF16), row(ffn_norm_g[0]))
    x2 = ffn(xn, h, 0, False)

    lam_init = 0.8 - 0.6 * math.exp(-0.3 * 1)
    wvt, ones_col = _value_weight_t(diff_w_qkv[0][:, 2 * DIFF_QK_WIDTH:], DIFF_HEADS, DIFF_V)
    q, k, vt = _diff_proj(x2, pos2, _inv_freq_lanes(DIFF_ROPE, DIFF_HEAD_DIM, 0),
                          row(attn_norm_g[1]), diff_w_qkv[0][:, :2 * DIFF_QK_WIDTH].astype(BF16),
                          wvt, ones_col)
    lam_params = jnp.stack([diff_lam_q1[0], diff_lam_k1[0], diff_lam_q2[0], diff_lam_k2[0]])
    o = _diff_attn(lam_params, q, k, vt, batch, seq, lam_init)
    xn, h = _out_proj(o, x2, diff_w_o[0].astype(BF16), row(ffn_norm_g[1]))
    x2 = ffn(xn, h, 1, True)
    return x2.reshape(batch, seq, D_MODEL)
```

```python
import functools
import math

import jax
import jax.numpy as jnp
from jax import lax
from jax.experimental import pallas as pl
from jax.experimental.pallas import tpu as pltpu

F32 = jnp.float32
BF16 = jnp.bfloat16

D_MODEL = 1024
ROPE_THETA = 500000.0
NORM_EPS = 1e-6

MLA_HEADS = 16
MLA_Q_LORA = 384
MLA_KV_LORA = 256
MLA_NOPE = 64
MLA_ROPE = 32
MLA_V = 64
MLA_QK = MLA_NOPE + MLA_ROPE

DIFF_HEADS = 8
DIFF_HEAD_DIM = 64
DIFF_V = 2 * DIFF_HEAD_DIM
DIFF_ROPE = DIFF_HEAD_DIM // 4
DIFF_QK_WIDTH = DIFF_HEADS * 2 * DIFF_HEAD_DIM
DIFF_V_WIDTH = DIFF_HEADS * DIFF_V

D_FF = 2816
CONV_WIDTH = 3

LANES = 128
BF16_SUBLANES = 16
VT_ONES_ROWS = BF16_SUBLANES
LOG2E = math.log2(math.e)

TOKEN_TILE = 512
FFN_TILE = 1024
Q_TILE = 512
KEY_CHUNK = 256
MLA_HEADS_PER_STEP = 4
DIFF_HEADS_PER_STEP = 2
FF_CHUNK = 256
VMEM_LIMIT = 48 * 1024 * 1024


def _rms(x):
    return x * lax.rsqrt(jnp.mean(x * x, axis=-1, keepdims=True) + NORM_EPS)


def _apply_rope(x, cos, sin_lo, sin_hi, half):
    return (x * cos + pltpu.roll(x, LANES - half, 1) * sin_lo
            + pltpu.roll(x, half, 1) * sin_hi)


def _rope_lane_tables(pos_ref, invf_ref, pieces):
    ang = invf_ref[...] * pos_ref[...].astype(F32)
    cos, sin = jnp.cos(ang), jnp.sin(ang)
    tm = ang.shape[1]
    cos_rows = [cos if p is True else jnp.ones((p, tm), F32) for p in pieces]
    sin_rows = [sin if p is True else jnp.zeros((p, tm), F32) for p in pieces]
    return jnp.concatenate(cos_rows, axis=0).T, jnp.concatenate(sin_rows, axis=0).T


def _mla_proj_kernel(x_ref, pos_ref, invf_ref, g_ref, wd_ref, qg_ref, kvg_ref, wuq_ref, wuk_ref,
                     wuvt_ref, ones_ref, q_ref, k_ref, vt_ref):
    h = _rms(x_ref[...]) * g_ref[...]
    down = jnp.dot(h.astype(BF16), wd_ref[...], preferred_element_type=F32)
    c_q = down[:, :MLA_Q_LORA]
    c_kv = down[:, MLA_Q_LORA:MLA_Q_LORA + MLA_KV_LORA]
    k_rope = down[:, MLA_Q_LORA + MLA_KV_LORA:]
    cqn = (_rms(c_q) * qg_ref[...]).astype(BF16)
    ckvn = (_rms(c_kv) * kvg_ref[...]).astype(BF16)
    q = jnp.dot(cqn, wuq_ref[...], preferred_element_type=F32)
    kn = jnp.dot(ckvn, wuk_ref[...], preferred_element_type=F32)
    vt = lax.dot_general(wuvt_ref[...], ckvn, (((1,), (1,)), ((), ())), preferred_element_type=F32)
    vt_ref[...] = (vt + ones_ref[...]).astype(BF16)

    half = MLA_ROPE // 2
    cos, sin = _rope_lane_tables(pos_ref, invf_ref, [MLA_NOPE, True, True, LANES - MLA_QK])
    lane = lax.broadcasted_iota(jnp.int32, cos.shape, 1)
    sin_lo = jnp.where(lane < MLA_NOPE + half, -sin, 0.0)
    sin_hi = jnp.where(lane >= MLA_NOPE + half, sin, 0.0)
    kr = _apply_rope(k_rope, cos, sin_lo, sin_hi, half)
    qs = (MLA_QK ** -0.5) * LOG2E
    cos_q, sin_lo_q, sin_hi_q = cos * qs, sin_lo * qs, sin_hi * qs
    for hd in range(MLA_HEADS):
        blk = slice(hd * LANES, (hd + 1) * LANES)
        q_ref[:, blk] = _apply_rope(q[:, blk], cos_q, sin_lo_q, sin_hi_q, half).astype(BF16)
        k_ref[:, blk] = (kn[:, blk] + kr).astype(BF16)


def _mla_proj(x2, pos2, invf, g, wd, qg, kvg, wuq, wuk, wuvt, ones_col):
    t = x2.shape[0]
    tm = TOKEN_TILE
    full = lambda a: pl.BlockSpec(a.shape, lambda i: (0, 0))
    vt_rows = wuvt.shape[0]
    return pl.pallas_call(
        _mla_proj_kernel,
        grid=(t // tm,),
        in_specs=[pl.BlockSpec((tm, D_MODEL), lambda i: (i, 0)),
                  pl.BlockSpec((None, 1, tm), lambda i: (i, 0, 0)),
                  full(invf), full(g), full(wd), full(qg), full(kvg), full(wuq), full(wuk),
                  full(wuvt), full(ones_col)],
        out_specs=[pl.BlockSpec((tm, MLA_HEADS * LANES), lambda i: (i, 0)),
                   pl.BlockSpec((tm, MLA_HEADS * LANES), lambda i: (i, 0)),
                   pl.BlockSpec((vt_rows, tm), lambda i: (0, i))],
        out_shape=[jax.ShapeDtypeStruct((t, MLA_HEADS * LANES), BF16),
                   jax.ShapeDtypeStruct((t, MLA_HEADS * LANES), BF16),
                   jax.ShapeDtypeStruct((vt_rows, t), BF16)],
        compiler_params=pltpu.CompilerParams(dimension_semantics=("parallel",),
                                             vmem_limit_bytes=VMEM_LIMIT),
        name="mla_proj",
    )(x2, pos2, invf, g, wd, qg, kvg, wuq, wuk, wuvt, ones_col)


def _attention_units(units, kq_fn, vt_fn, st_ref, n_val, seq, finish_fn):
    kc = KEY_CHUNK
    nchunk = seq // kc

    def scores_chunk(ui, c, m):
        k, q = kq_fn(units[ui], c)
        st = lax.dot_general(k, q, (((1,), (1,)), ((), ())), preferred_element_type=F32)
        st_ref[ui % 2, c * kc:(c + 1) * kc, :] = st
        mc = jnp.max(st, axis=0, keepdims=True)
        return mc if m is None else jnp.maximum(m, mc)

    m_next = None
    for c in range(nchunk):
        m_next = scores_chunk(0, c, m_next)
    for ui in range(len(units)):
        m_cur, m_next, r, pt_prev = m_next, None, None, None
        for c in range(nchunk + 1):
            if ui + 1 < len(units) and c < nchunk:
                m_next = scores_chunk(ui + 1, c, m_next)
            if pt_prev is not None:
                rc = jnp.dot(vt_fn(units[ui], c - 1), pt_prev, preferred_element_type=F32)
                r = rc if r is None else r + rc
            if c < nchunk:
                pt_prev = jnp.exp2(st_ref[ui % 2, c * kc:(c + 1) * kc, :] - m_cur).astype(BF16)
        finish_fn(units[ui], r[:n_val] / r[n_val:n_val + 1])


def _mla_attn_kernel(q_ref, k_ref, vt_ref, o_ref, st_ref):
    seq = q_ref.shape[0]
    tq = st_ref.shape[2]
    rows = MLA_V + VT_ONES_ROWS
    kc = KEY_CHUNK
    nh = MLA_HEADS_PER_STEP
    units = [(hh, qi) for qi in range(seq // tq) for hh in range(nh)]
    done = {}

    def kq_fn(unit, c):
        hh, qi = unit
        blk = slice(hh * LANES, (hh + 1) * LANES)
        return k_ref[c * kc:(c + 1) * kc, blk], q_ref[qi * tq:(qi + 1) * tq, blk]

    def vt_fn(unit, c):
        hh, _ = unit
        return vt_ref[hh * rows:(hh + 1) * rows, c * kc:(c + 1) * kc]

    def finish_fn(unit, ot):
        hh, qi = unit
        done[hh] = ot
        if hh == nh - 1:
            o_ref[qi * tq:(qi + 1) * tq, :] = jnp.concatenate(
                [done[i] for i in range(nh)], axis=0).T.astype(BF16)

    _attention_units(units, kq_fn, vt_fn, st_ref, MLA_V, seq, finish_fn)


def _mla_attn(q, k, vt, batch, seq):
    t = q.shape[0]
    nh = MLA_HEADS_PER_STEP
    rows = nh * (MLA_V + VT_ONES_ROWS)
    return pl.pallas_call(
        _mla_attn_kernel,
        grid=(batch, MLA_HEADS // nh),
        in_specs=[pl.BlockSpec((seq, nh * LANES), lambda b, h: (b, h)),
                  pl.BlockSpec((seq, nh * LANES), lambda b, h: (b, h)),
                  pl.BlockSpec((rows, seq), lambda b, h: (h, b))],
        out_specs=pl.BlockSpec((seq, nh * MLA_V), lambda b, h: (b, h)),
        out_shape=jax.ShapeDtypeStruct((t, MLA_HEADS * MLA_V), BF16),
        scratch_shapes=[pltpu.VMEM((2, seq, Q_TILE), F32)],
        compiler_params=pltpu.CompilerParams(
            dimension_semantics=("parallel", "parallel"), vmem_limit_bytes=VMEM_LIMIT),
        name="mla_attn",
    )(q, k, vt)


def _diff_proj_kernel(x_ref, pos_ref, invf_ref, g_ref, w_ref, wvt_ref, ones_ref, q_ref, k_ref,
                      vt_ref):
    h = (_rms(x_ref[...]) * g_ref[...]).astype(BF16)
    qkv = jnp.dot(h, w_ref[...], preferred_element_type=F32)
    vt = lax.dot_general(wvt_ref[...], h, (((1,), (1,)), ((), ())), preferred_element_type=F32)
    vt_ref[...] = (vt + ones_ref[...]).astype(BF16)

    half = DIFF_ROPE // 2
    rest = DIFF_HEAD_DIM - DIFF_ROPE
    cos, sin = _rope_lane_tables(pos_ref, invf_ref, [True, True, rest, True, True, rest])
    lane = lax.broadcasted_iota(jnp.int32, cos.shape, 1)
    sin_lo = jnp.where(lane % DIFF_HEAD_DIM < half, -sin, 0.0)
    sin_hi = jnp.where(lane % DIFF_HEAD_DIM >= half, sin, 0.0)
    qs = (DIFF_HEAD_DIM ** -0.5) * LOG2E
    cos_q, sin_lo_q, sin_hi_q = cos * qs, sin_lo * qs, sin_hi * qs
    for hd in range(DIFF_HEADS):
        blk = slice(hd * LANES, (hd + 1) * LANES)
        kblk = slice(DIFF_QK_WIDTH + hd * LANES, DIFF_QK_WIDTH + (hd + 1) * LANES)
        q_ref[:, blk] = _apply_rope(qkv[:, blk], cos_q, sin_lo_q, sin_hi_q, half).astype(BF16)
        k_ref[:, blk] = _apply_rope(qkv[:, kblk], cos, sin_lo, sin_hi, half).astype(BF16)


def _diff_proj(x2, pos2, invf, g, w, wvt, ones_col):
    t = x2.shape[0]
    tm = TOKEN_TILE
    vt_rows = wvt.shape[0]
    full = lambda a: pl.BlockSpec(a.shape, lambda i: (0, 0))
    row = lambda n: pl.BlockSpec((tm, n), lambda i: (i, 0))
    return pl.pallas_call(
        _diff_proj_kernel,
        grid=(t // tm,),
        in_specs=[row(D_MODEL), pl.BlockSpec((None, 1, tm), lambda i: (i, 0, 0)), full(invf),
                  full(g), full(w), full(wvt), full(ones_col)],
        out_specs=[row(DIFF_QK_WIDTH), row(DIFF_QK_WIDTH),
                   pl.BlockSpec((vt_rows, tm), lambda i: (0, i))],
        out_shape=[jax.ShapeDtypeStruct((t, DIFF_QK_WIDTH), BF16),
                   jax.ShapeDtypeStruct((t, DIFF_QK_WIDTH), BF16),
                   jax.ShapeDtypeStruct((vt_rows, t), BF16)],
        compiler_params=pltpu.CompilerParams(dimension_semantics=("parallel",),
                                             vmem_limit_bytes=VMEM_LIMIT),
        name="diff_proj",
    )(x2, pos2, invf, g, w, wvt, ones_col)


def _diff_attn_kernel(lam_ref, q_ref, k_ref, vt_ref, o_ref, st_ref, *, lam_init):
    seq = q_ref.shape[0]
    tq = st_ref.shape[2]
    kc = KEY_CHUNK
    nh = DIFF_HEADS_PER_STEP
    rows = DIFF_V + VT_ONES_ROWS
    lam_p = lam_ref[...]
    lam = (jnp.exp(jnp.sum(lam_p[0:1] * lam_p[1:2], axis=-1, keepdims=True))
           - jnp.exp(jnp.sum(lam_p[2:3] * lam_p[3:4], axis=-1, keepdims=True)) + lam_init)
    lane = lax.broadcasted_iota(jnp.int32, (tq, LANES), 1)
    units = [(hh, comp, qi) for qi in range(seq // tq) for hh in range(nh) for comp in range(2)]
    masked_q = {}
    comps = {}
    done = {}

    def kq_fn(unit, c):
        hh, comp, qi = unit
        blk = slice(hh * LANES, (hh + 1) * LANES)
        if unit not in masked_q:
            q = q_ref[qi * tq:(qi + 1) * tq, blk]
            in_comp = (lane < DIFF_HEAD_DIM) if comp == 0 else (lane >= DIFF_HEAD_DIM)
            masked_q[unit] = jnp.where(in_comp, q, jnp.zeros_like(q))
        return k_ref[c * kc:(c + 1) * kc, blk], masked_q[unit]

    def vt_fn(unit, c):
        hh = unit[0]
        return vt_ref[hh * rows:(hh + 1) * rows, c * kc:(c + 1) * kc]

    def finish_fn(unit, ot):
        hh, comp, qi = unit
        comps[comp] = ot
        if comp == 1:
            o = comps[0] - lam * comps[1]
            done[hh] = (o * lax.rsqrt(jnp.mean(o * o, axis=0, keepdims=True) + NORM_EPS)
                        * (1.0 - lam_init))
            if hh == nh - 1:
                o_ref[qi * tq:(qi + 1) * tq, :] = jnp.concatenate(
                    [done[i] for i in range(nh)], axis=0).T.astype(BF16)

    _attention_units(units, kq_fn, vt_fn, st_ref, DIFF_V, seq, finish_fn)


def _diff_attn(lam_params, q, k, vt, batch, seq, lam_init):
    t = q.shape[0]
    nh = DIFF_HEADS_PER_STEP
    return pl.pallas_call(
        functools.partial(_diff_attn_kernel, lam_init=lam_init),
        grid=(batch, DIFF_HEADS // nh),
        in_specs=[pl.BlockSpec(lam_params.shape, lambda b, h: (0, 0)),
                  pl.BlockSpec((seq, nh * LANES), lambda b, h: (b, h)),
                  pl.BlockSpec((seq, nh * LANES), lambda b, h: (b, h)),
                  pl.BlockSpec((nh * (DIFF_V + VT_ONES_ROWS), seq), lambda b, h: (h, b))],
        out_specs=pl.BlockSpec((seq, nh * LANES), lambda b, h: (b, h)),
        out_shape=jax.ShapeDtypeStruct((t, DIFF_V_WIDTH), BF16),
        scratch_shapes=[pltpu.VMEM((2, seq, Q_TILE), F32)],
        compiler_params=pltpu.CompilerParams(
            dimension_semantics=("parallel", "parallel"), vmem_limit_bytes=VMEM_LIMIT),
        name="diff_attn",
    )(lam_params, q, k, vt)


def _out_proj_kernel(o_ref, x_ref, w_ref, g_ref, xn_ref, h_ref):
    xn = x_ref[...] + jnp.dot(o_ref[...], w_ref[...], preferred_element_type=F32)
    xn_ref[...] = xn
    h_ref[...] = (_rms(xn) * g_ref[...]).astype(BF16)


def _out_proj(o, x2, w, g):
    t = x2.shape[0]
    tm = TOKEN_TILE
    full = lambda a: pl.BlockSpec(a.shape, lambda i: (0, 0))
    row = lambda n: pl.BlockSpec((tm, n), lambda i: (i, 0))
    return pl.pallas_call(
        _out_proj_kernel,
        grid=(t // tm,),
        in_specs=[row(o.shape[1]), row(D_MODEL), full(w), full(g)],
        out_specs=[row(D_MODEL), row(D_MODEL)],
        out_shape=[jax.ShapeDtypeStruct((t, D_MODEL), F32),
                   jax.ShapeDtypeStruct((t, D_MODEL), BF16)],
        compiler_params=pltpu.CompilerParams(dimension_semantics=("parallel",),
                                             vmem_limit_bytes=VMEM_LIMIT),
        name="out_proj",
    )(o, x2, w, g)


def _ffn_kernel(x_ref, h_ref, hprev_ref, hnext_ref, wg_ref, wv_ref, cwg_ref, cwv_ref, cbg_ref,
                cbv_ref, wdn_ref, fg_ref, out_ref, hext_ref, u_ref, acc_ref, *, nf, tiles_per_seq,
                apply_final_norm):
    s = pl.program_id(0)
    n_pairs = pl.num_programs(0) - 1
    up_pair = jnp.minimum(s, n_pairs - 1)
    up_tile, up_chunk = up_pair // nf, up_pair % nf
    chunk = jnp.maximum(s - 1, 0) % nf
    tm = x_ref.shape[0]
    hb = BF16_SUBLANES

    @pl.when(s == 0)
    def _():
        u_ref[1] = jnp.zeros(u_ref.shape[1:], F32)
        acc_ref[...] = jnp.zeros_like(acc_ref)

    @pl.when((up_chunk == 0) & (s < n_pairs))
    def _():
        at_start = (up_tile % tiles_per_seq) == 0
        at_end = (up_tile % tiles_per_seq) == tiles_per_seq - 1
        hprev = hprev_ref[...]
        hnext = hnext_ref[...]
        hext_ref[0:hb, :] = jnp.where(at_start, jnp.zeros_like(hprev), hprev)
        hext_ref[hb:hb + tm, :] = h_ref[...]
        hext_ref[hb + tm:, :] = jnp.where(at_end, jnp.zeros_like(hnext), hnext)

    @pl.when((s > 0) & (chunk == 0))
    def _():
        acc_ref[...] = x_ref[...]

    prev = (s + 1) % 2

    def conv(j, cw_ref, cb_ref):
        cols = slice((j % 2) * LANES, (j % 2 + 1) * LANES)
        cw = cw_ref[...]
        return (u_ref[prev, j, hb - 1:hb - 1 + tm, :] * cw[0:1, cols]
                + u_ref[prev, j, hb:hb + tm, :] * cw[1:2, cols]
                + u_ref[prev, j, hb + 1:hb + 1 + tm, :] * cw[2:3, cols] + cb_ref[:, cols])

    gate = jnp.concatenate([conv(0, cwg_ref, cbg_ref), conv(1, cwg_ref, cbg_ref)], axis=1)
    val = jnp.concatenate([conv(2, cwv_ref, cbv_ref), conv(3, cwv_ref, cbv_ref)], axis=1)
    act = (gate * jax.nn.sigmoid(gate) * val).astype(BF16)
    acc_ref[...] += jnp.dot(act, wdn_ref[...], preferred_element_type=F32)

    hx = hext_ref[...]
    ug = jnp.dot(hx, wg_ref[...], preferred_element_type=F32)
    uv = jnp.dot(hx, wv_ref[...], preferred_element_type=F32)
    for j in range(2):
        u_ref[s % 2, j] = ug[:, j * LANES:(j + 1) * LANES]
        u_ref[s % 2, 2 + j] = uv[:, j * LANES:(j + 1) * LANES]

    @pl.when((s > 0) & (chunk == nf - 1))
    def _():
        y = acc_ref[...]
        if apply_final_norm:
            y = _rms(y) * fg_ref[...]
        out_ref[...] = y


def _ffn(xn, h, wup, cw, cb, wdn, fg, seq, apply_final_norm):
    t = xn.shape[0]
    tm = FFN_TILE
    fc = FF_CHUNK
    hb = BF16_SUBLANES
    nf = D_FF // fc
    n_pairs = (t // tm) * nf
    halo_per_tile = tm // hb
    last_halo = t // hb - 1
    up_tile = lambda s: jnp.minimum(s, n_pairs - 1) // nf
    up_chunk = lambda s: jnp.minimum(s, n_pairs - 1) % nf
    tile = lambda s: jnp.maximum(s - 1, 0) // nf
    chunk = lambda s: jnp.maximum(s - 1, 0) % nf
    return pl.pallas_call(
        functools.partial(_ffn_kernel, nf=nf, tiles_per_seq=seq // tm,
                          apply_final_norm=apply_final_norm),
        grid=(n_pairs + 1,),
        in_specs=[pl.BlockSpec((tm, D_MODEL), lambda s: (tile(s), 0)),
                  pl.BlockSpec((tm, D_MODEL), lambda s: (up_tile(s), 0)),
                  pl.BlockSpec((hb, D_MODEL),
                               lambda s: (jnp.maximum(up_tile(s) * halo_per_tile - 1, 0), 0)),
                  pl.BlockSpec((hb, D_MODEL),
                               lambda s: (jnp.minimum((up_tile(s) + 1) * halo_per_tile, last_halo), 0)),
                  pl.BlockSpec((D_MODEL, fc), lambda s: (0, up_chunk(s))),
                  pl.BlockSpec((D_MODEL, fc), lambda s: (0, nf + up_chunk(s))),
                  pl.BlockSpec((CONV_WIDTH, fc), lambda s: (0, chunk(s))),
                  pl.BlockSpec((CONV_WIDTH, fc), lambda s: (0, nf + chunk(s))),
                  pl.BlockSpec((1, fc), lambda s: (0, chunk(s))),
                  pl.BlockSpec((1, fc), lambda s: (0, nf + chunk(s))),
                  pl.BlockSpec((fc, D_MODEL), lambda s: (chunk(s), 0)),
                  pl.BlockSpec((1, D_MODEL), lambda s: (0, 0))],
        out_specs=pl.BlockSpec((tm, D_MODEL), lambda s: (tile(s), 0)),
        out_shape=jax.ShapeDtypeStruct((t, D_MODEL), F32),
        scratch_shapes=[pltpu.VMEM((tm + 2 * hb, D_MODEL), BF16),
                        pltpu.VMEM((2, 4, tm + 2 * hb, LANES), F32),
                        pltpu.VMEM((tm, D_MODEL), F32)],
        compiler_params=pltpu.CompilerParams(dimension_semantics=("arbitrary",),
                                             vmem_limit_bytes=VMEM_LIMIT),
        name="conv_ffn",
    )(xn, h, h, h, wup, wup, cw, cw, cb, cb, wdn, fg)


def _inv_freq_col(rot_dim):
    return (ROPE_THETA ** (-jnp.arange(0, rot_dim, 2, dtype=F32) / rot_dim))[:, None]


def _pad_heads(w, heads, width):
    w = w.reshape(w.shape[0], heads, width)
    return jnp.pad(w, ((0, 0), (0, 0), (0, LANES - width))).reshape(w.shape[0], heads * LANES)


def _value_weight_t(w, heads, width):
    wt = w.T.reshape(heads, width, w.shape[0])
    wt = jnp.pad(wt, ((0, 0), (0, VT_ONES_ROWS), (0, 0))).reshape(-1, w.shape[0])
    ones = jnp.pad(jnp.zeros((heads, width, 1), F32), ((0, 0), (0, VT_ONES_ROWS), (0, 0)),
                   constant_values=1.0).reshape(-1, 1)
    return wt.astype(BF16), ones


def kernel(x, positions, attn_norm_g, ffn_norm_g, final_norm_g, mla_w_down, mla_q_norm_g,
           mla_kv_norm_g, mla_w_uq, mla_w_ukv, mla_w_o, diff_w_qkv, diff_lam_q1, diff_lam_k1,
           diff_lam_q2, diff_lam_k2, diff_w_o, ffn_w_up, ffn_conv_w, ffn_conv_b, ffn_w_down):
    batch, seq, _ = x.shape
    t = batch * seq
    x2 = x.reshape(t, D_MODEL)
    pos2 = positions.reshape(t // TOKEN_TILE, 1, TOKEN_TILE)
    row = lambda a: a.reshape(1, -1)

    def ffn(xn, h, layer, final):
        return _ffn(xn, h, ffn_w_up[layer].astype(BF16), ffn_conv_w[layer],
                    row(ffn_conv_b[layer]), ffn_w_down[layer].astype(BF16), row(final_norm_g),
                    seq, final)

    wd = mla_w_down[0]
    k_rope_cols = jnp.pad(wd[:, MLA_Q_LORA + MLA_KV_LORA:], ((0, 0), (MLA_NOPE, LANES - MLA_QK)))
    wd_p = jnp.concatenate([wd[:, :MLA_Q_LORA + MLA_KV_LORA], k_rope_cols], axis=1).astype(BF16)
    wukv = mla_w_ukv[0].reshape(MLA_KV_LORA, MLA_HEADS, MLA_NOPE + MLA_V)
    wuk_p = _pad_heads(wukv[:, :, :MLA_NOPE].reshape(MLA_KV_LORA, -1), MLA_HEADS, MLA_NOPE)
    wuvt, ones_col = _value_weight_t(wukv[:, :, MLA_NOPE:].reshape(MLA_KV_LORA, -1), MLA_HEADS,
                                     MLA_V)
    q, k, vt = _mla_proj(x2, pos2, _inv_freq_col(MLA_ROPE), row(attn_norm_g[0]),
                         wd_p, row(mla_q_norm_g[0]), row(mla_kv_norm_g[0]),
                         _pad_heads(mla_w_uq[0], MLA_HEADS, MLA_QK).astype(BF16),
                         wuk_p.astype(BF16), wuvt, ones_col)
    o = _mla_attn(q, k, vt, batch, seq)
    xn, h = _out_proj(o, x2, mla_w_o[0].astype(BF16), row(ffn_norm_g[0]))
    x2 = ffn(xn, h, 0, False)

    lam_init = 0.8 - 0.6 * math.exp(-0.3 * 1)
    wvt, ones_col = _value_weight_t(diff_w_qkv[0][:, 2 * DIFF_QK_WIDTH:], DIFF_HEADS, DIFF_V)
    q, k, vt = _diff_proj(x2, pos2, _inv_freq_col(DIFF_ROPE),
                          row(attn_norm_g[1]), diff_w_qkv[0][:, :2 * DIFF_QK_WIDTH].astype(BF16),
                          wvt, ones_col)
    lam_params = jnp.stack([diff_lam_q1[0], diff_lam_k1[0], diff_lam_q2[0], diff_lam_k2[0]])
    o = _diff_attn(lam_params, q, k, vt, batch, seq, lam_init)
    xn, h = _out_proj(o, x2, diff_w_o[0].astype(BF16), row(ffn_norm_g[1]))
    x2 = ffn(xn, h, 1, True)
    return x2.reshape(batch, seq, D_MODEL)
```

```python
import functools
import math

import jax
import jax.numpy as jnp
from jax import lax
from jax.experimental import pallas as pl
from jax.experimental.pallas import tpu as pltpu

F32 = jnp.float32
BF16 = jnp.bfloat16

D_MODEL = 1024
ROPE_THETA = 500000.0
NORM_EPS = 1e-6

MLA_HEADS = 16
MLA_Q_LORA = 384
MLA_KV_LORA = 256
MLA_NOPE = 64
MLA_ROPE = 32
MLA_V = 64
MLA_QK = MLA_NOPE + MLA_ROPE

DIFF_HEADS = 8
DIFF_HEAD_DIM = 64
DIFF_V = 2 * DIFF_HEAD_DIM
DIFF_ROPE = DIFF_HEAD_DIM // 4
DIFF_QK_WIDTH = DIFF_HEADS * 2 * DIFF_HEAD_DIM
DIFF_V_WIDTH = DIFF_HEADS * DIFF_V

D_FF = 2816
CONV_WIDTH = 3

LANES = 128
BF16_SUBLANES = 16
VT_ONES_ROWS = BF16_SUBLANES
LOG2E = math.log2(math.e)

TOKEN_TILE = 512
FFN_TILE = 1024
Q_TILE = 512
KEY_CHUNK = 256
MLA_HEADS_PER_STEP = 4
DIFF_HEADS_PER_STEP = 2
FF_CHUNK = 256
VMEM_LIMIT = 48 * 1024 * 1024
FFN_VMEM_LIMIT = 56 * 1024 * 1024


def _rms(x):
    return x * lax.rsqrt(jnp.mean(x * x, axis=-1, keepdims=True) + NORM_EPS)


def _apply_rope(x, cos, sin_lo, sin_hi, half):
    return (x * cos + pltpu.roll(x, LANES - half, 1) * sin_lo
            + pltpu.roll(x, half, 1) * sin_hi)


def _rope_lane_tables(pos_ref, invf_ref, pieces):
    ang = invf_ref[...] * pos_ref[...].astype(F32)
    cos, sin = jnp.cos(ang), jnp.sin(ang)
    tm = ang.shape[1]
    cos_rows = [cos if p is True else jnp.ones((p, tm), F32) for p in pieces]
    sin_rows = [sin if p is True else jnp.zeros((p, tm), F32) for p in pieces]
    return jnp.concatenate(cos_rows, axis=0).T, jnp.concatenate(sin_rows, axis=0).T


def _mla_proj_kernel(x_ref, pos_ref, invf_ref, g_ref, wd_ref, qg_ref, kvg_ref, wuq_ref, wuk_ref,
                     wuvt_ref, ones_ref, q_ref, k_ref, vt_ref):
    h = _rms(x_ref[...]) * g_ref[...]
    down = jnp.dot(h.astype(BF16), wd_ref[...], preferred_element_type=F32)
    c_q = down[:, :MLA_Q_LORA]
    c_kv = down[:, MLA_Q_LORA:MLA_Q_LORA + MLA_KV_LORA]
    k_rope = down[:, MLA_Q_LORA + MLA_KV_LORA:]
    cqn = (_rms(c_q) * qg_ref[...]).astype(BF16)
    ckvn = (_rms(c_kv) * kvg_ref[...]).astype(BF16)
    q = jnp.dot(cqn, wuq_ref[...], preferred_element_type=F32)
    kn = jnp.dot(ckvn, wuk_ref[...], preferred_element_type=F32)
    vt = lax.dot_general(wuvt_ref[...], ckvn, (((1,), (1,)), ((), ())), preferred_element_type=F32)
    vt_ref[...] = (vt + ones_ref[...]).astype(BF16)

    half = MLA_ROPE // 2
    cos, sin = _rope_lane_tables(pos_ref, invf_ref, [MLA_NOPE, True, True, LANES - MLA_QK])
    lane = lax.broadcasted_iota(jnp.int32, cos.shape, 1)
    sin_lo = jnp.where(lane < MLA_NOPE + half, -sin, 0.0)
    sin_hi = jnp.where(lane >= MLA_NOPE + half, sin, 0.0)
    kr = _apply_rope(k_rope, cos, sin_lo, sin_hi, half)
    qs = (MLA_QK ** -0.5) * LOG2E
    cos_q, sin_lo_q, sin_hi_q = cos * qs, sin_lo * qs, sin_hi * qs
    for hd in range(MLA_HEADS):
        blk = slice(hd * LANES, (hd + 1) * LANES)
        q_ref[:, blk] = _apply_rope(q[:, blk], cos_q, sin_lo_q, sin_hi_q, half).astype(BF16)
        k_ref[:, blk] = (kn[:, blk] + kr).astype(BF16)


def _mla_proj(x2, pos2, invf, g, wd, qg, kvg, wuq, wuk, wuvt, ones_col):
    t = x2.shape[0]
    tm = TOKEN_TILE
    full = lambda a: pl.BlockSpec(a.shape, lambda i: (0, 0))
    vt_rows = wuvt.shape[0]
    return pl.pallas_call(
        _mla_proj_kernel,
        grid=(t // tm,),
        in_specs=[pl.BlockSpec((tm, D_MODEL), lambda i: (i, 0)),
                  pl.BlockSpec((None, 1, tm), lambda i: (i, 0, 0)),
                  full(invf), full(g), full(wd), full(qg), full(kvg), full(wuq), full(wuk),
                  full(wuvt), full(ones_col)],
        out_specs=[pl.BlockSpec((tm, MLA_HEADS * LANES), lambda i: (i, 0)),
                   pl.BlockSpec((tm, MLA_HEADS * LANES), lambda i: (i, 0)),
                   pl.BlockSpec((vt_rows, tm), lambda i: (0, i))],
        out_shape=[jax.ShapeDtypeStruct((t, MLA_HEADS * LANES), BF16),
                   jax.ShapeDtypeStruct((t, MLA_HEADS * LANES), BF16),
                   jax.ShapeDtypeStruct((vt_rows, t), BF16)],
        compiler_params=pltpu.CompilerParams(dimension_semantics=("parallel",),
                                             vmem_limit_bytes=VMEM_LIMIT),
        name="mla_proj",
    )(x2, pos2, invf, g, wd, qg, kvg, wuq, wuk, wuvt, ones_col)


def _attention_units(units, kq_fn, vt_fn, st_ref, n_val, seq, finish_fn):
    kc = KEY_CHUNK
    nchunk = seq // kc

    def scores_chunk(ui, c, m):
        k, q = kq_fn(units[ui], c)
        st = lax.dot_general(k, q, (((1,), (1,)), ((), ())), preferred_element_type=F32)
        st_ref[ui % 2, c * kc:(c + 1) * kc, :] = st
        mc = jnp.max(st, axis=0, keepdims=True)
        return mc if m is None else jnp.maximum(m, mc)

    m_next = None
    for c in range(nchunk):
        m_next = scores_chunk(0, c, m_next)
    for ui in range(len(units)):
        m_cur, m_next, r, pt_prev = m_next, None, None, None
        for c in range(nchunk + 1):
            if ui + 1 < len(units) and c < nchunk:
                m_next = scores_chunk(ui + 1, c, m_next)
            if pt_prev is not None:
                rc = jnp.dot(vt_fn(units[ui], c - 1), pt_prev, preferred_element_type=F32)
                r = rc if r is None else r + rc
            if c < nchunk:
                pt_prev = jnp.exp2(st_ref[ui % 2, c * kc:(c + 1) * kc, :] - m_cur).astype(BF16)
        finish_fn(units[ui], r[:n_val] / r[n_val:n_val + 1])


def _mla_attn_kernel(q_ref, k_ref, vt_ref, o_ref, st_ref):
    seq = q_ref.shape[0]
    tq = st_ref.shape[2]
    rows = MLA_V + VT_ONES_ROWS
    kc = KEY_CHUNK
    nh = MLA_HEADS_PER_STEP
    units = [(hh, qi) for qi in range(seq // tq) for hh in range(nh)]
    done = {}

    def kq_fn(unit, c):
        hh, qi = unit
        blk = slice(hh * LANES, (hh + 1) * LANES)
        return k_ref[c * kc:(c + 1) * kc, blk], q_ref[qi * tq:(qi + 1) * tq, blk]

    def vt_fn(unit, c):
        hh, _ = unit
        return vt_ref[hh * rows:(hh + 1) * rows, c * kc:(c + 1) * kc]

    def finish_fn(unit, ot):
        hh, qi = unit
        done[hh] = ot
        if hh == nh - 1:
            o_ref[qi * tq:(qi + 1) * tq, :] = jnp.concatenate(
                [done[i] for i in range(nh)], axis=0).T.astype(BF16)

    _attention_units(units, kq_fn, vt_fn, st_ref, MLA_V, seq, finish_fn)


def _mla_attn(q, k, vt, batch, seq):
    t = q.shape[0]
    nh = MLA_HEADS_PER_STEP
    rows = nh * (MLA_V + VT_ONES_ROWS)
    return pl.pallas_call(
        _mla_attn_kernel,
        grid=(batch, MLA_HEADS // nh),
        in_specs=[pl.BlockSpec((seq, nh * LANES), lambda b, h: (b, h)),
                  pl.BlockSpec((seq, nh * LANES), lambda b, h: (b, h)),
                  pl.BlockSpec((rows, seq), lambda b, h: (h, b))],
        out_specs=pl.BlockSpec((seq, nh * MLA_V), lambda b, h: (b, h)),
        out_shape=jax.ShapeDtypeStruct((t, MLA_HEADS * MLA_V), BF16),
        scratch_shapes=[pltpu.VMEM((2, seq, Q_TILE), F32)],
        compiler_params=pltpu.CompilerParams(
            dimension_semantics=("parallel", "parallel"), vmem_limit_bytes=VMEM_LIMIT),
        name="mla_attn",
    )(q, k, vt)


def _diff_proj_kernel(x_ref, pos_ref, invf_ref, g_ref, w_ref, wvt_ref, ones_ref, q_ref, k_ref,
                      vt_ref):
    h = (_rms(x_ref[...]) * g_ref[...]).astype(BF16)
    qkv = jnp.dot(h, w_ref[...], preferred_element_type=F32)
    vt = lax.dot_general(wvt_ref[...], h, (((1,), (1,)), ((), ())), preferred_element_type=F32)
    vt_ref[...] = (vt + ones_ref[...]).astype(BF16)

    half = DIFF_ROPE // 2
    rest = DIFF_HEAD_DIM - DIFF_ROPE
    cos, sin = _rope_lane_tables(pos_ref, invf_ref, [True, True, rest, True, True, rest])
    lane = lax.broadcasted_iota(jnp.int32, cos.shape, 1)
    sin_lo = jnp.where(lane % DIFF_HEAD_DIM < half, -sin, 0.0)
    sin_hi = jnp.where(lane % DIFF_HEAD_DIM >= half, sin, 0.0)
    qs = (DIFF_HEAD_DIM ** -0.5) * LOG2E
    cos_q, sin_lo_q, sin_hi_q = cos * qs, sin_lo * qs, sin_hi * qs
    for hd in range(DIFF_HEADS):
        blk = slice(hd * LANES, (hd + 1) * LANES)
        kblk = slice(DIFF_QK_WIDTH + hd * LANES, DIFF_QK_WIDTH + (hd + 1) * LANES)
        q_ref[:, blk] = _apply_rope(qkv[:, blk], cos_q, sin_lo_q, sin_hi_q, half).astype(BF16)
        k_ref[:, blk] = _apply_rope(qkv[:, kblk], cos, sin_lo, sin_hi, half).astype(BF16)


def _diff_proj(x2, pos2, invf, g, w, wvt, ones_col):
    t = x2.shape[0]
    tm = TOKEN_TILE
    vt_rows = wvt.shape[0]
    full = lambda a: pl.BlockSpec(a.shape, lambda i: (0, 0))
    row = lambda n: pl.BlockSpec((tm, n), lambda i: (i, 0))
    return pl.pallas_call(
        _diff_proj_kernel,
        grid=(t // tm,),
        in_specs=[row(D_MODEL), pl.BlockSpec((None, 1, tm), lambda i: (i, 0, 0)), full(invf),
                  full(g), full(w), full(wvt), full(ones_col)],
        out_specs=[row(DIFF_QK_WIDTH), row(DIFF_QK_WIDTH),
                   pl.BlockSpec((vt_rows, tm), lambda i: (0, i))],
        out_shape=[jax.ShapeDtypeStruct((t, DIFF_QK_WIDTH), BF16),
                   jax.ShapeDtypeStruct((t, DIFF_QK_WIDTH), BF16),
                   jax.ShapeDtypeStruct((vt_rows, t), BF16)],
        compiler_params=pltpu.CompilerParams(dimension_semantics=("parallel",),
                                             vmem_limit_bytes=VMEM_LIMIT),
        name="diff_proj",
    )(x2, pos2, invf, g, w, wvt, ones_col)


def _diff_attn_kernel(lam_ref, q_ref, k_ref, vt_ref, o_ref, st_ref, *, lam_init):
    seq = q_ref.shape[0]
    tq = st_ref.shape[2]
    kc = KEY_CHUNK
    nh = DIFF_HEADS_PER_STEP
    rows = DIFF_V + VT_ONES_ROWS
    lam_p = lam_ref[...]
    lam = (jnp.exp(jnp.sum(lam_p[0:1] * lam_p[1:2], axis=-1, keepdims=True))
           - jnp.exp(jnp.sum(lam_p[2:3] * lam_p[3:4], axis=-1, keepdims=True)) + lam_init)
    lane = lax.broadcasted_iota(jnp.int32, (tq, LANES), 1)
    units = [(hh, comp, qi) for qi in range(seq // tq) for hh in range(nh) for comp in range(2)]
    masked_q = {}
    comps = {}
    done = {}

    def kq_fn(unit, c):
        hh, comp, qi = unit
        blk = slice(hh * LANES, (hh + 1) * LANES)
        if unit not in masked_q:
            q = q_ref[qi * tq:(qi + 1) * tq, blk]
            in_comp = (lane < DIFF_HEAD_DIM) if comp == 0 else (lane >= DIFF_HEAD_DIM)
            masked_q[unit] = jnp.where(in_comp, q, jnp.zeros_like(q))
        return k_ref[c * kc:(c + 1) * kc, blk], masked_q[unit]

    def vt_fn(unit, c):
        hh = unit[0]
        return vt_ref[hh * rows:(hh + 1) * rows, c * kc:(c + 1) * kc]

    def finish_fn(unit, ot):
        hh, comp, qi = unit
        comps[comp] = ot
        if comp == 1:
            o = comps[0] - lam * comps[1]
            done[hh] = (o * lax.rsqrt(jnp.mean(o * o, axis=0, keepdims=True) + NORM_EPS)
                        * (1.0 - lam_init))
            if hh == nh - 1:
                o_ref[qi * tq:(qi + 1) * tq, :] = jnp.concatenate(
                    [done[i] for i in range(nh)], axis=0).T.astype(BF16)

    _attention_units(units, kq_fn, vt_fn, st_ref, DIFF_V, seq, finish_fn)


def _diff_attn(lam_params, q, k, vt, batch, seq, lam_init):
    t = q.shape[0]
    nh = DIFF_HEADS_PER_STEP
    return pl.pallas_call(
        functools.partial(_diff_attn_kernel, lam_init=lam_init),
        grid=(batch, DIFF_HEADS // nh),
        in_specs=[pl.BlockSpec(lam_params.shape, lambda b, h: (0, 0)),
                  pl.BlockSpec((seq, nh * LANES), lambda b, h: (b, h)),
                  pl.BlockSpec((seq, nh * LANES), lambda b, h: (b, h)),
                  pl.BlockSpec((nh * (DIFF_V + VT_ONES_ROWS), seq), lambda b, h: (h, b))],
        out_specs=pl.BlockSpec((seq, nh * LANES), lambda b, h: (b, h)),
        out_shape=jax.ShapeDtypeStruct((t, DIFF_V_WIDTH), BF16),
        scratch_shapes=[pltpu.VMEM((2, seq, Q_TILE), F32)],
        compiler_params=pltpu.CompilerParams(
            dimension_semantics=("parallel", "parallel"), vmem_limit_bytes=VMEM_LIMIT),
        name="diff_attn",
    )(lam_params, q, k, vt)


def _ffn_kernel(o_ref, oprev_ref, onext_ref, x_ref, xprev_ref, xnext_ref, wo_ref, gf_ref, wg_ref,
                wv_ref, cwg_ref, cwv_ref, cbg_ref, cbv_ref, wdn_ref, fg_ref, out_ref, hext_ref,
                u_ref, xn_ref, acc_ref, *, nf, tiles_per_seq, apply_final_norm):
    s = pl.program_id(0)
    n_pairs = pl.num_programs(0) - 1
    up_pair = jnp.minimum(s, n_pairs - 1)
    up_tile, up_chunk = up_pair // nf, up_pair % nf
    chunk = jnp.maximum(s - 1, 0) % nf
    tm = x_ref.shape[0]
    hb = BF16_SUBLANES

    @pl.when(s == 0)
    def _():
        u_ref[1] = jnp.zeros(u_ref.shape[1:], F32)
        acc_ref[...] = jnp.zeros_like(acc_ref)

    @pl.when((up_chunk == 0) & (s < n_pairs))
    def _():
        at_start = (up_tile % tiles_per_seq) == 0
        at_end = (up_tile % tiles_per_seq) == tiles_per_seq - 1
        o_ext = jnp.concatenate([oprev_ref[...], o_ref[...], onext_ref[...]], axis=0)
        d = jnp.dot(o_ext, wo_ref[...], preferred_element_type=F32)
        g = gf_ref[...]
        xn = x_ref[...] + d[hb:hb + tm]
        xn_ref[...] = xn
        hext_ref[hb:hb + tm, :] = (_rms(xn) * g).astype(BF16)
        hprev = (_rms(xprev_ref[...] + d[:hb]) * g).astype(BF16)
        hnext = (_rms(xnext_ref[...] + d[hb + tm:]) * g).astype(BF16)
        hext_ref[0:hb, :] = jnp.where(at_start, jnp.zeros_like(hprev), hprev)
        hext_ref[hb + tm:, :] = jnp.where(at_end, jnp.zeros_like(hnext), hnext)

    @pl.when((s > 0) & (chunk == 0))
    def _():
        acc_ref[...] = xn_ref[...]

    prev = (s + 1) % 2

    def conv(j, cw_ref, cb_ref):
        cols = slice((j % 2) * LANES, (j % 2 + 1) * LANES)
        cw = cw_ref[...]
        return (u_ref[prev, j, hb - 1:hb - 1 + tm, :] * cw[0:1, cols]
                + u_ref[prev, j, hb:hb + tm, :] * cw[1:2, cols]
                + u_ref[prev, j, hb + 1:hb + 1 + tm, :] * cw[2:3, cols] + cb_ref[:, cols])

    gate = jnp.concatenate([conv(0, cwg_ref, cbg_ref), conv(1, cwg_ref, cbg_ref)], axis=1)
    val = jnp.concatenate([conv(2, cwv_ref, cbv_ref), conv(3, cwv_ref, cbv_ref)], axis=1)
    act = (gate * jax.nn.sigmoid(gate) * val).astype(BF16)
    acc_ref[...] += jnp.dot(act, wdn_ref[...], preferred_element_type=F32)

    hx = hext_ref[...]
    ug = jnp.dot(hx, wg_ref[...], preferred_element_type=F32)
    uv = jnp.dot(hx, wv_ref[...], preferred_element_type=F32)
    for j in range(2):
        u_ref[s % 2, j] = ug[:, j * LANES:(j + 1) * LANES]
        u_ref[s % 2, 2 + j] = uv[:, j * LANES:(j + 1) * LANES]

    @pl.when((s > 0) & (chunk == nf - 1))
    def _():
        y = acc_ref[...]
        if apply_final_norm:
            y = _rms(y) * fg_ref[...]
        out_ref[...] = y


def _ffn(o, x2, wo, gf, wup, cw, cb, wdn, fg, seq, apply_final_norm):
    t = x2.shape[0]
    tm = FFN_TILE
    fc = FF_CHUNK
    hb = BF16_SUBLANES
    nf = D_FF // fc
    n_pairs = (t // tm) * nf
    halo_per_tile = tm // hb
    last_halo = t // hb - 1
    up_tile = lambda s: jnp.minimum(s, n_pairs - 1) // nf
    up_chunk = lambda s: jnp.minimum(s, n_pairs - 1) % nf
    tile = lambda s: jnp.maximum(s - 1, 0) // nf
    chunk = lambda s: jnp.maximum(s - 1, 0) % nf
    main = pl.BlockSpec((tm, D_MODEL), lambda s: (up_tile(s), 0))
    prev_halo = pl.BlockSpec((hb, D_MODEL),
                             lambda s: (jnp.maximum(up_tile(s) * halo_per_tile - 1, 0), 0))
    next_halo = pl.BlockSpec((hb, D_MODEL),
                             lambda s: (jnp.minimum((up_tile(s) + 1) * halo_per_tile, last_halo), 0))
    return pl.pallas_call(
        functools.partial(_ffn_kernel, nf=nf, tiles_per_seq=seq // tm,
                          apply_final_norm=apply_final_norm),
        grid=(n_pairs + 1,),
        in_specs=[main, prev_halo, next_halo, main, prev_halo, next_halo,
                  pl.BlockSpec((D_MODEL, D_MODEL), lambda s: (0, 0)),
                  pl.BlockSpec((1, D_MODEL), lambda s: (0, 0)),
                  pl.BlockSpec((D_MODEL, fc), lambda s: (0, up_chunk(s))),
                  pl.BlockSpec((D_MODEL, fc), lambda s: (0, nf + up_chunk(s))),
                  pl.BlockSpec((CONV_WIDTH, fc), lambda s: (0, chunk(s))),
                  pl.BlockSpec((CONV_WIDTH, fc), lambda s: (0, nf + chunk(s))),
                  pl.BlockSpec((1, fc), lambda s: (0, chunk(s))),
                  pl.BlockSpec((1, fc), lambda s: (0, nf + chunk(s))),
                  pl.BlockSpec((fc, D_MODEL), lambda s: (chunk(s), 0)),
                  pl.BlockSpec((1, D_MODEL), lambda s: (0, 0))],
        out_specs=pl.BlockSpec((tm, D_MODEL), lambda s: (tile(s), 0)),
        out_shape=jax.ShapeDtypeStruct((t, D_MODEL), F32),
        scratch_shapes=[pltpu.VMEM((tm + 2 * hb, D_MODEL), BF16),
                        pltpu.VMEM((2, 4, tm + 2 * hb, LANES), F32),
                        pltpu.VMEM((tm, D_MODEL), F32),
                        pltpu.VMEM((tm, D_MODEL), F32)],
        compiler_params=pltpu.CompilerParams(dimension_semantics=("arbitrary",),
                                             vmem_limit_bytes=FFN_VMEM_LIMIT),
        name="conv_ffn",
    )(o, o, o, x2, x2, x2, wo, gf, wup, wup, cw, cw, cb, cb, wdn, fg)


def _inv_freq_col(rot_dim):
    return (ROPE_THETA ** (-jnp.arange(0, rot_dim, 2, dtype=F32) / rot_dim))[:, None]


def _pad_heads(w, heads, width):
    w = w.reshape(w.shape[0], heads, width)
    return jnp.pad(w, ((0, 0), (0, 0), (0, LANES - width))).reshape(w.shape[0], heads * LANES)


def _value_weight_t(w, heads, width):
    wt = w.T.reshape(heads, width, w.shape[0])
    wt = jnp.pad(wt, ((0, 0), (0, VT_ONES_ROWS), (0, 0))).reshape(-1, w.shape[0])
    ones = jnp.pad(jnp.zeros((heads, width, 1), F32), ((0, 0), (0, VT_ONES_ROWS), (0, 0)),
                   constant_values=1.0).reshape(-1, 1)
    return wt.astype(BF16), ones


def kernel(x, positions, attn_norm_g, ffn_norm_g, final_norm_g, mla_w_down, mla_q_norm_g,
           mla_kv_norm_g, mla_w_uq, mla_w_ukv, mla_w_o, diff_w_qkv, diff_lam_q1, diff_lam_k1,
           diff_lam_q2, diff_lam_k2, diff_w_o, ffn_w_up, ffn_conv_w, ffn_conv_b, ffn_w_down):
    batch, seq, _ = x.shape
    t = batch * seq
    x2 = x.reshape(t, D_MODEL)
    pos2 = positions.reshape(t // TOKEN_TILE, 1, TOKEN_TILE)
    row = lambda a: a.reshape(1, -1)

    def out_proj_ffn(o, x2, w_o, layer, final):
        return _ffn(o, x2, w_o.astype(BF16), row(ffn_norm_g[layer]),
                    ffn_w_up[layer].astype(BF16), ffn_conv_w[layer], row(ffn_conv_b[layer]),
                    ffn_w_down[layer].astype(BF16), row(final_norm_g), seq, final)

    wd = mla_w_down[0]
    k_rope_cols = jnp.pad(wd[:, MLA_Q_LORA + MLA_KV_LORA:], ((0, 0), (MLA_NOPE, LANES - MLA_QK)))
    wd_p = jnp.concatenate([wd[:, :MLA_Q_LORA + MLA_KV_LORA], k_rope_cols], axis=1).astype(BF16)
    wukv = mla_w_ukv[0].reshape(MLA_KV_LORA, MLA_HEADS, MLA_NOPE + MLA_V)
    wuk_p = _pad_heads(wukv[:, :, :MLA_NOPE].reshape(MLA_KV_LORA, -1), MLA_HEADS, MLA_NOPE)
    wuvt, ones_col = _value_weight_t(wukv[:, :, MLA_NOPE:].reshape(MLA_KV_LORA, -1), MLA_HEADS,
                                     MLA_V)
    q, k, vt = _mla_proj(x2, pos2, _inv_freq_col(MLA_ROPE), row(attn_norm_g[0]),
                         wd_p, row(mla_q_norm_g[0]), row(mla_kv_norm_g[0]),
                         _pad_heads(mla_w_uq[0], MLA_HEADS, MLA_QK).astype(BF16),
                         wuk_p.astype(BF16), wuvt, ones_col)
    o = _mla_attn(q, k, vt, batch, seq)
    x2 = out_proj_ffn(o, x2, mla_w_o[0], 0, False)

    lam_init = 0.8 - 0.6 * math.exp(-0.3 * 1)
    wvt, ones_col = _value_weight_t(diff_w_qkv[0][:, 2 * DIFF_QK_WIDTH:], DIFF_HEADS, DIFF_V)
    q, k, vt = _diff_proj(x2, pos2, _inv_freq_col(DIFF_ROPE),
                          row(attn_norm_g[1]), diff_w_qkv[0][:, :2 * DIFF_QK_WIDTH].astype(BF16),
                          wvt, ones_col)
    lam_params = jnp.stack([diff_lam_q1[0], diff_lam_k1[0], diff_lam_q2[0], diff_lam_k2[0]])
    o = _diff_attn(lam_params, q, k, vt, batch, seq, lam_init)
    x2 = out_proj_ffn(o, x2, diff_w_o[0], 1, True)
    return x2.reshape(batch, seq, D_MODEL)
```

```python
import functools
import math

import jax
import jax.numpy as jnp
from jax import lax
from jax.experimental import pallas as pl
from jax.experimental.pallas import tpu as pltpu

F32 = jnp.float32
BF16 = jnp.bfloat16

D_MODEL = 1024
ROPE_THETA = 500000.0
NORM_EPS = 1e-6

MLA_HEADS = 16
MLA_Q_LORA = 384
MLA_KV_LORA = 256
MLA_NOPE = 64
MLA_ROPE = 32
MLA_V = 64
MLA_QK = MLA_NOPE + MLA_ROPE

DIFF_HEADS = 8
DIFF_HEAD_DIM = 64
DIFF_V = 2 * DIFF_HEAD_DIM
DIFF_ROPE = DIFF_HEAD_DIM // 4
DIFF_QK_WIDTH = DIFF_HEADS * 2 * DIFF_HEAD_DIM
DIFF_V_WIDTH = DIFF_HEADS * DIFF_V

D_FF = 2816
CONV_WIDTH = 3

LANES = 128
BF16_SUBLANES = 16
VT_ONES_ROWS = BF16_SUBLANES
LOG2E = math.log2(math.e)

TOKEN_TILE = 512
FFN_TILE = 1024
Q_TILE = 1024
KEY_CHUNK = 256
MLA_HEADS_PER_STEP = 4
DIFF_HEADS_PER_STEP = 2
FF_CHUNK = 256
VMEM_LIMIT = 48 * 1024 * 1024
FFN_VMEM_LIMIT = 56 * 1024 * 1024


def _rms(x):
    return x * lax.rsqrt(jnp.mean(x * x, axis=-1, keepdims=True) + NORM_EPS)


def _apply_rope(x, cos, sin_lo, sin_hi, half):
    return (x * cos + pltpu.roll(x, LANES - half, 1) * sin_lo
            + pltpu.roll(x, half, 1) * sin_hi)


def _rope_lane_tables(pos_ref, invf_ref, pieces):
    ang = invf_ref[...] * pos_ref[...].astype(F32)
    cos, sin = jnp.cos(ang), jnp.sin(ang)
    tm = ang.shape[1]
    cos_rows = [cos if p is True else jnp.ones((p, tm), F32) for p in pieces]
    sin_rows = [sin if p is True else jnp.zeros((p, tm), F32) for p in pieces]
    return jnp.concatenate(cos_rows, axis=0).T, jnp.concatenate(sin_rows, axis=0).T


def _mla_proj_kernel(x_ref, pos_ref, invf_ref, g_ref, wd_ref, qg_ref, kvg_ref, wuq_ref, wuk_ref,
                     wuvt_ref, ones_ref, q_ref, k_ref, vt_ref):
    h = _rms(x_ref[...]) * g_ref[...]
    down = jnp.dot(h.astype(BF16), wd_ref[...], preferred_element_type=F32)
    c_q = down[:, :MLA_Q_LORA]
    c_kv = down[:, MLA_Q_LORA:MLA_Q_LORA + MLA_KV_LORA]
    k_rope = down[:, MLA_Q_LORA + MLA_KV_LORA:]
    cqn = (_rms(c_q) * qg_ref[...]).astype(BF16)
    ckvn = (_rms(c_kv) * kvg_ref[...]).astype(BF16)
    q = jnp.dot(cqn, wuq_ref[...], preferred_element_type=F32)
    kn = jnp.dot(ckvn, wuk_ref[...], preferred_element_type=F32)
    vt = lax.dot_general(wuvt_ref[...], ckvn, (((1,), (1,)), ((), ())), preferred_element_type=F32)
    vt_ref[...] = (vt + ones_ref[...]).astype(BF16)

    half = MLA_ROPE // 2
    cos, sin = _rope_lane_tables(pos_ref, invf_ref, [MLA_NOPE, True, True, LANES - MLA_QK])
    lane = lax.broadcasted_iota(jnp.int32, cos.shape, 1)
    sin_lo = jnp.where(lane < MLA_NOPE + half, -sin, 0.0)
    sin_hi = jnp.where(lane >= MLA_NOPE + half, sin, 0.0)
    kr = _apply_rope(k_rope, cos, sin_lo, sin_hi, half)
    qs = (MLA_QK ** -0.5) * LOG2E
    cos_q, sin_lo_q, sin_hi_q = cos * qs, sin_lo * qs, sin_hi * qs
    for hd in range(MLA_HEADS):
        blk = slice(hd * LANES, (hd + 1) * LANES)
        q_ref[:, blk] = _apply_rope(q[:, blk], cos_q, sin_lo_q, sin_hi_q, half).astype(BF16)
        k_ref[:, blk] = (kn[:, blk] + kr).astype(BF16)


def _mla_proj(x2, pos2, invf, g, wd, qg, kvg, wuq, wuk, wuvt, ones_col):
    t = x2.shape[0]
    tm = TOKEN_TILE
    full = lambda a: pl.BlockSpec(a.shape, lambda i: (0, 0))
    vt_rows = wuvt.shape[0]
    return pl.pallas_call(
        _mla_proj_kernel,
        grid=(t // tm,),
        in_specs=[pl.BlockSpec((tm, D_MODEL), lambda i: (i, 0)),
                  pl.BlockSpec((None, 1, tm), lambda i: (i, 0, 0)),
                  full(invf), full(g), full(wd), full(qg), full(kvg), full(wuq), full(wuk),
                  full(wuvt), full(ones_col)],
        out_specs=[pl.BlockSpec((tm, MLA_HEADS * LANES), lambda i: (i, 0)),
                   pl.BlockSpec((tm, MLA_HEADS * LANES), lambda i: (i, 0)),
                   pl.BlockSpec((vt_rows, tm), lambda i: (0, i))],
        out_shape=[jax.ShapeDtypeStruct((t, MLA_HEADS * LANES), BF16),
                   jax.ShapeDtypeStruct((t, MLA_HEADS * LANES), BF16),
                   jax.ShapeDtypeStruct((vt_rows, t), BF16)],
        compiler_params=pltpu.CompilerParams(dimension_semantics=("parallel",),
                                             vmem_limit_bytes=VMEM_LIMIT),
        name="mla_proj",
    )(x2, pos2, invf, g, wd, qg, kvg, wuq, wuk, wuvt, ones_col)


def _attention_units(units, kq_fn, vt_fn, st_ref, n_val, seq, finish_fn):
    kc = KEY_CHUNK
    nchunk = seq // kc

    def scores_chunk(ui, c, m):
        k, q = kq_fn(units[ui], c)
        st = lax.dot_general(k, q, (((1,), (1,)), ((), ())), preferred_element_type=F32)
        st_ref[ui % 2, c * kc:(c + 1) * kc, :] = st
        mc = jnp.max(st, axis=0, keepdims=True)
        return mc if m is None else jnp.maximum(m, mc)

    m_next = None
    for c in range(nchunk):
        m_next = scores_chunk(0, c, m_next)
    for ui in range(len(units)):
        m_cur, m_next, r, pt_prev = m_next, None, None, None
        for c in range(nchunk + 1):
            if ui + 1 < len(units) and c < nchunk:
                m_next = scores_chunk(ui + 1, c, m_next)
            if pt_prev is not None:
                rc = jnp.dot(vt_fn(units[ui], c - 1), pt_prev, preferred_element_type=F32)
                r = rc if r is None else r + rc
            if c < nchunk:
                pt_prev = jnp.exp2(st_ref[ui % 2, c * kc:(c + 1) * kc, :] - m_cur).astype(BF16)
        finish_fn(units[ui], r[:n_val] / r[n_val:n_val + 1])


def _mla_attn_kernel(q_ref, k_ref, vt_ref, o_ref, st_ref):
    seq = q_ref.shape[0]
    tq = st_ref.shape[2]
    rows = MLA_V + VT_ONES_ROWS
    kc = KEY_CHUNK
    nh = MLA_HEADS_PER_STEP
    units = [(hh, qi) for qi in range(seq // tq) for hh in range(nh)]
    done = {}

    def kq_fn(unit, c):
        hh, qi = unit
        blk = slice(hh * LANES, (hh + 1) * LANES)
        return k_ref[c * kc:(c + 1) * kc, blk], q_ref[qi * tq:(qi + 1) * tq, blk]

    def vt_fn(unit, c):
        hh, _ = unit
        return vt_ref[hh * rows:(hh + 1) * rows, c * kc:(c + 1) * kc]

    def finish_fn(unit, ot):
        hh, qi = unit
        done[hh] = ot
        if hh == nh - 1:
            o_ref[qi * tq:(qi + 1) * tq, :] = jnp.concatenate(
                [done[i] for i in range(nh)], axis=0).T.astype(BF16)

    _attention_units(units, kq_fn, vt_fn, st_ref, MLA_V, seq, finish_fn)


def _mla_attn(q, k, vt, batch, seq):
    t = q.shape[0]
    nh = MLA_HEADS_PER_STEP
    rows = nh * (MLA_V + VT_ONES_ROWS)
    return pl.pallas_call(
        _mla_attn_kernel,
        grid=(batch, MLA_HEADS // nh),
        in_specs=[pl.BlockSpec((seq, nh * LANES), lambda b, h: (b, h)),
                  pl.BlockSpec((seq, nh * LANES), lambda b, h: (b, h)),
                  pl.BlockSpec((rows, seq), lambda b, h: (h, b))],
        out_specs=pl.BlockSpec((seq, nh * MLA_V), lambda b, h: (b, h)),
        out_shape=jax.ShapeDtypeStruct((t, MLA_HEADS * MLA_V), BF16),
        scratch_shapes=[pltpu.VMEM((2, seq, Q_TILE), F32)],
        compiler_params=pltpu.CompilerParams(
            dimension_semantics=("parallel", "parallel"), vmem_limit_bytes=VMEM_LIMIT),
        name="mla_attn",
    )(q, k, vt)


def _diff_proj_kernel(x_ref, pos_ref, invf_ref, g_ref, w_ref, wvt_ref, ones_ref, q_ref, k_ref,
                      vt_ref):
    h = (_rms(x_ref[...]) * g_ref[...]).astype(BF16)
    qkv = jnp.dot(h, w_ref[...], preferred_element_type=F32)
    vt = lax.dot_general(wvt_ref[...], h, (((1,), (1,)), ((), ())), preferred_element_type=F32)
    vt_ref[...] = (vt + ones_ref[...]).astype(BF16)

    half = DIFF_ROPE // 2
    rest = DIFF_HEAD_DIM - DIFF_ROPE
    cos, sin = _rope_lane_tables(pos_ref, invf_ref, [True, True, rest, True, True, rest])
    lane = lax.broadcasted_iota(jnp.int32, cos.shape, 1)
    sin_lo = jnp.where(lane % DIFF_HEAD_DIM < half, -sin, 0.0)
    sin_hi = jnp.where(lane % DIFF_HEAD_DIM >= half, sin, 0.0)
    qs = (DIFF_HEAD_DIM ** -0.5) * LOG2E
    cos_q, sin_lo_q, sin_hi_q = cos * qs, sin_lo * qs, sin_hi * qs
    for hd in range(DIFF_HEADS):
        blk = slice(hd * LANES, (hd + 1) * LANES)
        kblk = slice(DIFF_QK_WIDTH + hd * LANES, DIFF_QK_WIDTH + (hd + 1) * LANES)
        q_ref[:, blk] = _apply_rope(qkv[:, blk], cos_q, sin_lo_q, sin_hi_q, half).astype(BF16)
        k_ref[:, blk] = _apply_rope(qkv[:, kblk], cos, sin_lo, sin_hi, half).astype(BF16)


def _diff_proj(x2, pos2, invf, g, w, wvt, ones_col):
    t = x2.shape[0]
    tm = TOKEN_TILE
    vt_rows = wvt.shape[0]
    full = lambda a: pl.BlockSpec(a.shape, lambda i: (0, 0))
    row = lambda n: pl.BlockSpec((tm, n), lambda i: (i, 0))
    return pl.pallas_call(
        _diff_proj_kernel,
        grid=(t // tm,),
        in_specs=[row(D_MODEL), pl.BlockSpec((None, 1, tm), lambda i: (i, 0, 0)), full(invf),
                  full(g), full(w), full(wvt), full(ones_col)],
        out_specs=[row(DIFF_QK_WIDTH), row(DIFF_QK_WIDTH),
                   pl.BlockSpec((vt_rows, tm), lambda i: (0, i))],
        out_shape=[jax.ShapeDtypeStruct((t, DIFF_QK_WIDTH), BF16),
                   jax.ShapeDtypeStruct((t, DIFF_QK_WIDTH), BF16),
                   jax.ShapeDtypeStruct((vt_rows, t), BF16)],
        compiler_params=pltpu.CompilerParams(dimension_semantics=("parallel",),
                                             vmem_limit_bytes=VMEM_LIMIT),
        name="diff_proj",
    )(x2, pos2, invf, g, w, wvt, ones_col)


def _diff_attn_kernel(lam_ref, q_ref, k_ref, vt_ref, o_ref, st_ref, *, lam_init):
    seq = q_ref.shape[0]
    tq = st_ref.shape[2]
    kc = KEY_CHUNK
    nh = DIFF_HEADS_PER_STEP
    rows = DIFF_V + VT_ONES_ROWS
    lam_p = lam_ref[...]
    lam = (jnp.exp(jnp.sum(lam_p[0:1] * lam_p[1:2], axis=-1, keepdims=True))
           - jnp.exp(jnp.sum(lam_p[2:3] * lam_p[3:4], axis=-1, keepdims=True)) + lam_init)
    lane = lax.broadcasted_iota(jnp.int32, (tq, LANES), 1)
    units = [(hh, comp, qi) for qi in range(seq // tq) for hh in range(nh) for comp in range(2)]
    masked_q = {}
    comps = {}
    done = {}

    def kq_fn(unit, c):
        hh, comp, qi = unit
        blk = slice(hh * LANES, (hh + 1) * LANES)
        if unit not in masked_q:
            q = q_ref[qi * tq:(qi + 1) * tq, blk]
            in_comp = (lane < DIFF_HEAD_DIM) if comp == 0 else (lane >= DIFF_HEAD_DIM)
            masked_q[unit] = jnp.where(in_comp, q, jnp.zeros_like(q))
        return k_ref[c * kc:(c + 1) * kc, blk], masked_q[unit]

    def vt_fn(unit, c):
        hh = unit[0]
        return vt_ref[hh * rows:(hh + 1) * rows, c * kc:(c + 1) * kc]

    def finish_fn(unit, ot):
        hh, comp, qi = unit
        comps[comp] = ot
        if comp == 1:
            o = comps[0] - lam * comps[1]
            done[hh] = (o * lax.rsqrt(jnp.mean(o * o, axis=0, keepdims=True) + NORM_EPS)
                        * (1.0 - lam_init))
            if hh == nh - 1:
                o_ref[qi * tq:(qi + 1) * tq, :] = jnp.concatenate(
                    [done[i] for i in range(nh)], axis=0).T.astype(BF16)

    _attention_units(units, kq_fn, vt_fn, st_ref, DIFF_V, seq, finish_fn)


def _diff_attn(lam_params, q, k, vt, batch, seq, lam_init):
    t = q.shape[0]
    nh = DIFF_HEADS_PER_STEP
    return pl.pallas_call(
        functools.partial(_diff_attn_kernel, lam_init=lam_init),
        grid=(batch, DIFF_HEADS // nh),
        in_specs=[pl.BlockSpec(lam_params.shape, lambda b, h: (0, 0)),
                  pl.BlockSpec((seq, nh * LANES), lambda b, h: (b, h)),
                  pl.BlockSpec((seq, nh * LANES), lambda b, h: (b, h)),
                  pl.BlockSpec((nh * (DIFF_V + VT_ONES_ROWS), seq), lambda b, h: (h, b))],
        out_specs=pl.BlockSpec((seq, nh * LANES), lambda b, h: (b, h)),
        out_shape=jax.ShapeDtypeStruct((t, DIFF_V_WIDTH), BF16),
        scratch_shapes=[pltpu.VMEM((2, seq, Q_TILE), F32)],
        compiler_params=pltpu.CompilerParams(
            dimension_semantics=("parallel", "parallel"), vmem_limit_bytes=VMEM_LIMIT),
        name="diff_attn",
    )(lam_params, q, k, vt)


def _ffn_kernel(o_ref, oprev_ref, onext_ref, x_ref, xprev_ref, xnext_ref, wo_ref, gf_ref, wg_ref,
                wv_ref, cwg_ref, cwv_ref, cbg_ref, cbv_ref, wdn_ref, fg_ref, out_ref, hext_ref,
                u_ref, xn_ref, acc_ref, *, nf, tiles_per_seq, apply_final_norm):
    s = pl.program_id(0)
    n_pairs = pl.num_programs(0) - 1
    up_pair = jnp.minimum(s, n_pairs - 1)
    up_tile, up_chunk = up_pair // nf, up_pair % nf
    chunk = jnp.maximum(s - 1, 0) % nf
    tm = x_ref.shape[0]
    hb = BF16_SUBLANES

    @pl.when(s == 0)
    def _():
        u_ref[1] = jnp.zeros(u_ref.shape[1:], F32)
        acc_ref[...] = jnp.zeros_like(acc_ref)

    @pl.when((up_chunk == 0) & (s < n_pairs))
    def _():
        at_start = (up_tile % tiles_per_seq) == 0
        at_end = (up_tile % tiles_per_seq) == tiles_per_seq - 1
        o_ext = jnp.concatenate([oprev_ref[...], o_ref[...], onext_ref[...]], axis=0)
        d = jnp.dot(o_ext, wo_ref[...], preferred_element_type=F32)
        g = gf_ref[...]
        xn = x_ref[...] + d[hb:hb + tm]
        xn_ref[...] = xn
        hext_ref[hb:hb + tm, :] = (_rms(xn) * g).astype(BF16)
        hprev = (_rms(xprev_ref[...] + d[:hb]) * g).astype(BF16)
        hnext = (_rms(xnext_ref[...] + d[hb + tm:]) * g).astype(BF16)
        hext_ref[0:hb, :] = jnp.where(at_start, jnp.zeros_like(hprev), hprev)
        hext_ref[hb + tm:, :] = jnp.where(at_end, jnp.zeros_like(hnext), hnext)

    @pl.when((s > 0) & (chunk == 0))
    def _():
        acc_ref[...] = xn_ref[...]

    prev = (s + 1) % 2

    def conv(j, cw_ref, cb_ref):
        cols = slice((j % 2) * LANES, (j % 2 + 1) * LANES)
        cw = cw_ref[...]
        return (u_ref[prev, j, hb - 1:hb - 1 + tm, :] * cw[0:1, cols]
                + u_ref[prev, j, hb:hb + tm, :] * cw[1:2, cols]
                + u_ref[prev, j, hb + 1:hb + 1 + tm, :] * cw[2:3, cols] + cb_ref[:, cols])

    gate = jnp.concatenate([conv(0, cwg_ref, cbg_ref), conv(1, cwg_ref, cbg_ref)], axis=1)
    val = jnp.concatenate([conv(2, cwv_ref, cbv_ref), conv(3, cwv_ref, cbv_ref)], axis=1)
    act = (gate * jax.nn.sigmoid(gate) * val).astype(BF16)
    acc_ref[...] += jnp.dot(act, wdn_ref[...], preferred_element_type=F32)

    hx = hext_ref[...]
    ug = jnp.dot(hx, wg_ref[...], preferred_element_type=F32)
    uv = jnp.dot(hx, wv_ref[...], preferred_element_type=F32)
    for j in range(2):
        u_ref[s % 2, j] = ug[:, j * LANES:(j + 1) * LANES]
        u_ref[s % 2, 2 + j] = uv[:, j * LANES:(j + 1) * LANES]

    @pl.when((s > 0) & (chunk == nf - 1))
    def _():
        y = acc_ref[...]
        if apply_final_norm:
            y = _rms(y) * fg_ref[...]
        out_ref[...] = y


def _ffn(o, x2, wo, gf, wup, cw, cb, wdn, fg, seq, apply_final_norm):
    t = x2.shape[0]
    tm = FFN_TILE
    fc = FF_CHUNK
    hb = BF16_SUBLANES
    nf = D_FF // fc
    n_pairs = (t // tm) * nf
    halo_per_tile = tm // hb
    last_halo = t // hb - 1
    up_tile = lambda s: jnp.minimum(s, n_pairs - 1) // nf
    up_chunk = lambda s: jnp.minimum(s, n_pairs - 1) % nf
    tile = lambda s: jnp.maximum(s - 1, 0) // nf
    chunk = lambda s: jnp.maximum(s - 1, 0) % nf
    main = pl.BlockSpec((tm, D_MODEL), lambda s: (up_tile(s), 0))
    prev_halo = pl.BlockSpec((hb, D_MODEL),
                             lambda s: (jnp.maximum(up_tile(s) * halo_per_tile - 1, 0), 0))
    next_halo = pl.BlockSpec((hb, D_MODEL),
                             lambda s: (jnp.minimum((up_tile(s) + 1) * halo_per_tile, last_halo), 0))
    return pl.pallas_call(
        functools.partial(_ffn_kernel, nf=nf, tiles_per_seq=seq // tm,
                          apply_final_norm=apply_final_norm),
        grid=(n_pairs + 1,),
        in_specs=[main, prev_halo, next_halo, main, prev_halo, next_halo,
                  pl.BlockSpec((D_MODEL, D_MODEL), lambda s: (0, 0)),
                  pl.BlockSpec((1, D_MODEL), lambda s: (0, 0)),
                  pl.BlockSpec((D_MODEL, fc), lambda s: (0, up_chunk(s))),
                  pl.BlockSpec((D_MODEL, fc), lambda s: (0, nf + up_chunk(s))),
                  pl.BlockSpec((CONV_WIDTH, fc), lambda s: (0, chunk(s))),
                  pl.BlockSpec((CONV_WIDTH, fc), lambda s: (0, nf + chunk(s))),
                  pl.BlockSpec((1, fc), lambda s: (0, chunk(s))),
                  pl.BlockSpec((1, fc), lambda s: (0, nf + chunk(s))),
                  pl.BlockSpec((fc, D_MODEL), lambda s: (chunk(s), 0)),
                  pl.BlockSpec((1, D_MODEL), lambda s: (0, 0))],
        out_specs=pl.BlockSpec((tm, D_MODEL), lambda s: (tile(s), 0)),
        out_shape=jax.ShapeDtypeStruct((t, D_MODEL), F32),
        scratch_shapes=[pltpu.VMEM((tm + 2 * hb, D_MODEL), BF16),
                        pltpu.VMEM((2, 4, tm + 2 * hb, LANES), F32),
                        pltpu.VMEM((tm, D_MODEL), F32),
                        pltpu.VMEM((tm, D_MODEL), F32)],
        compiler_params=pltpu.CompilerParams(dimension_semantics=("arbitrary",),
                                             vmem_limit_bytes=FFN_VMEM_LIMIT),
        name="conv_ffn",
    )(o, o, o, x2, x2, x2, wo, gf, wup, wup, cw, cw, cb, cb, wdn, fg)


def _inv_freq_col(rot_dim):
    return (ROPE_THETA ** (-jnp.arange(0, rot_dim, 2, dtype=F32) / rot_dim))[:, None]


def _pad_heads(w, heads, width):
    w = w.reshape(w.shape[0], heads, width)
    return jnp.pad(w, ((0, 0), (0, 0), (0, LANES - width))).reshape(w.shape[0], heads * LANES)


def _value_weight_t(w, heads, width):
    wt = w.T.reshape(heads, width, w.shape[0])
    wt = jnp.pad(wt, ((0, 0), (0, VT_ONES_ROWS), (0, 0))).reshape(-1, w.shape[0])
    ones = jnp.pad(jnp.zeros((heads, width, 1), F32), ((0, 0), (0, VT_ONES_ROWS), (0, 0)),
                   constant_values=1.0).reshape(-1, 1)
    return wt.astype(BF16), ones


def kernel(x, positions, attn_norm_g, ffn_norm_g, final_norm_g, mla_w_down, mla_q_norm_g,
           mla_kv_norm_g, mla_w_uq, mla_w_ukv, mla_w_o, diff_w_qkv, diff_lam_q1, diff_lam_k1,
           diff_lam_q2, diff_lam_k2, diff_w_o, ffn_w_up, ffn_conv_w, ffn_conv_b, ffn_w_down):
    batch, seq, _ = x.shape
    t = batch * seq
    x2 = x.reshape(t, D_MODEL)
    pos2 = positions.reshape(t // TOKEN_TILE, 1, TOKEN_TILE)
    row = lambda a: a.reshape(1, -1)

    def out_proj_ffn(o, x2, w_o, layer, final):
        return _ffn(o, x2, w_o.astype(BF16), row(ffn_norm_g[layer]),
                    ffn_w_up[layer].astype(BF16), ffn_conv_w[layer], row(ffn_conv_b[layer]),
                    ffn_w_down[layer].astype(BF16), row(final_norm_g), seq, final)

    wd = mla_w_down[0]
    k_rope_cols = jnp.pad(wd[:, MLA_Q_LORA + MLA_KV_LORA:], ((0, 0), (MLA_NOPE, LANES - MLA_QK)))
    wd_p = jnp.concatenate([wd[:, :MLA_Q_LORA + MLA_KV_LORA], k_rope_cols], axis=1).astype(BF16)
    wukv = mla_w_ukv[0].reshape(MLA_KV_LORA, MLA_HEADS, MLA_NOPE + MLA_V)
    wuk_p = _pad_heads(wukv[:, :, :MLA_NOPE].reshape(MLA_KV_LORA, -1), MLA_HEADS, MLA_NOPE)
    wuvt, ones_col = _value_weight_t(wukv[:, :, MLA_NOPE:].reshape(MLA_KV_LORA, -1), MLA_HEADS,
                                     MLA_V)
    q, k, vt = _mla_proj(x2, pos2, _inv_freq_col(MLA_ROPE), row(attn_norm_g[0]),
                         wd_p, row(mla_q_norm_g[0]), row(mla_kv_norm_g[0]),
                         _pad_heads(mla_w_uq[0], MLA_HEADS, MLA_QK).astype(BF16),
                         wuk_p.astype(BF16), wuvt, ones_col)
    o = _mla_attn(q, k, vt, batch, seq)
    x2 = out_proj_ffn(o, x2, mla_w_o[0], 0, False)

    lam_init = 0.8 - 0.6 * math.exp(-0.3 * 1)
    wvt, ones_col = _value_weight_t(diff_w_qkv[0][:, 2 * DIFF_QK_WIDTH:], DIFF_HEADS, DIFF_V)
    q, k, vt = _diff_proj(x2, pos2, _inv_freq_col(DIFF_ROPE),
                          row(attn_norm_g[1]), diff_w_qkv[0][:, :2 * DIFF_QK_WIDTH].astype(BF16),
                          wvt, ones_col)
    lam_params = jnp.stack([diff_lam_q1[0], diff_lam_k1[0], diff_lam_q2[0], diff_lam_k2[0]])
    o = _diff_attn(lam_params, q, k, vt, batch, seq, lam_init)
    x2 = out_proj_ffn(o, x2, diff_w_o[0], 1, True)
    return x2.reshape(batch, seq, D_MODEL)
```

```python
import functools
import math

import jax
import jax.numpy as jnp
from jax import lax
from jax.experimental import pallas as pl
from jax.experimental.pallas import tpu as pltpu

F32 = jnp.float32
BF16 = jnp.bfloat16

D_MODEL = 1024
ROPE_THETA = 500000.0
NORM_EPS = 1e-6

MLA_HEADS = 16
MLA_Q_LORA = 384
MLA_KV_LORA = 256
MLA_NOPE = 64
MLA_ROPE = 32
MLA_V = 64
MLA_QK = MLA_NOPE + MLA_ROPE

DIFF_HEADS = 8
DIFF_HEAD_DIM = 64
DIFF_V = 2 * DIFF_HEAD_DIM
DIFF_ROPE = DIFF_HEAD_DIM // 4
DIFF_QK_WIDTH = DIFF_HEADS * 2 * DIFF_HEAD_DIM
DIFF_V_WIDTH = DIFF_HEADS * DIFF_V

D_FF = 2816
CONV_WIDTH = 3

LANES = 128
BF16_SUBLANES = 16
VT_ONES_ROWS = BF16_SUBLANES
LOG2E = math.log2(math.e)

TOKEN_TILE = 512
FFN_TILE = 1024
Q_TILE = 2048
KEY_CHUNK = 256
MLA_HEADS_PER_STEP = 4
DIFF_HEADS_PER_STEP = 2
FF_CHUNK = 256
VMEM_LIMIT = 48 * 1024 * 1024
FFN_VMEM_LIMIT = 56 * 1024 * 1024


def _rms(x):
    return x * lax.rsqrt(jnp.mean(x * x, axis=-1, keepdims=True) + NORM_EPS)


def _apply_rope(x, cos, sin_lo, sin_hi, half):
    return (x * cos + pltpu.roll(x, LANES - half, 1) * sin_lo
            + pltpu.roll(x, half, 1) * sin_hi)


def _rope_lane_tables(pos_ref, invf_ref, pieces):
    ang = invf_ref[...] * pos_ref[...].astype(F32)
    cos, sin = jnp.cos(ang), jnp.sin(ang)
    tm = ang.shape[1]
    cos_rows = [cos if p is True else jnp.ones((p, tm), F32) for p in pieces]
    sin_rows = [sin if p is True else jnp.zeros((p, tm), F32) for p in pieces]
    return jnp.concatenate(cos_rows, axis=0).T, jnp.concatenate(sin_rows, axis=0).T


def _mla_proj_kernel(x_ref, pos_ref, invf_ref, g_ref, wd_ref, qg_ref, kvg_ref, wuq_ref, wuk_ref,
                     wuvt_ref, ones_ref, q_ref, k_ref, vt_ref):
    h = _rms(x_ref[...]) * g_ref[...]
    down = jnp.dot(h.astype(BF16), wd_ref[...], preferred_element_type=F32)
    c_q = down[:, :MLA_Q_LORA]
    c_kv = down[:, MLA_Q_LORA:MLA_Q_LORA + MLA_KV_LORA]
    k_rope = down[:, MLA_Q_LORA + MLA_KV_LORA:]
    cqn = (_rms(c_q) * qg_ref[...]).astype(BF16)
    ckvn = (_rms(c_kv) * kvg_ref[...]).astype(BF16)
    q = jnp.dot(cqn, wuq_ref[...], preferred_element_type=F32)
    kn = jnp.dot(ckvn, wuk_ref[...], preferred_element_type=F32)
    vt = lax.dot_general(wuvt_ref[...], ckvn, (((1,), (1,)), ((), ())), preferred_element_type=F32)
    vt_ref[...] = (vt + ones_ref[...]).astype(BF16)

    half = MLA_ROPE // 2
    cos, sin = _rope_lane_tables(pos_ref, invf_ref, [MLA_NOPE, True, True, LANES - MLA_QK])
    lane = lax.broadcasted_iota(jnp.int32, cos.shape, 1)
    sin_lo = jnp.where(lane < MLA_NOPE + half, -sin, 0.0)
    sin_hi = jnp.where(lane >= MLA_NOPE + half, sin, 0.0)
    kr = _apply_rope(k_rope, cos, sin_lo, sin_hi, half)
    qs = (MLA_QK ** -0.5) * LOG2E
    cos_q, sin_lo_q, sin_hi_q = cos * qs, sin_lo * qs, sin_hi * qs
    for hd in range(MLA_HEADS):
        blk = slice(hd * LANES, (hd + 1) * LANES)
        q_ref[:, blk] = _apply_rope(q[:, blk], cos_q, sin_lo_q, sin_hi_q, half).astype(BF16)
        k_ref[:, blk] = (kn[:, blk] + kr).astype(BF16)


def _mla_proj(x2, pos2, invf, g, wd, qg, kvg, wuq, wuk, wuvt, ones_col):
    t = x2.shape[0]
    tm = TOKEN_TILE
    full = lambda a: pl.BlockSpec(a.shape, lambda i: (0, 0))
    vt_rows = wuvt.shape[0]
    return pl.pallas_call(
        _mla_proj_kernel,
        grid=(t // tm,),
        in_specs=[pl.BlockSpec((tm, D_MODEL), lambda i: (i, 0)),
                  pl.BlockSpec((None, 1, tm), lambda i: (i, 0, 0)),
                  full(invf), full(g), full(wd), full(qg), full(kvg), full(wuq), full(wuk),
                  full(wuvt), full(ones_col)],
        out_specs=[pl.BlockSpec((tm, MLA_HEADS * LANES), lambda i: (i, 0)),
                   pl.BlockSpec((tm, MLA_HEADS * LANES), lambda i: (i, 0)),
                   pl.BlockSpec((vt_rows, tm), lambda i: (0, i))],
        out_shape=[jax.ShapeDtypeStruct((t, MLA_HEADS * LANES), BF16),
                   jax.ShapeDtypeStruct((t, MLA_HEADS * LANES), BF16),
                   jax.ShapeDtypeStruct((vt_rows, t), BF16)],
        compiler_params=pltpu.CompilerParams(dimension_semantics=("parallel",),
                                             vmem_limit_bytes=VMEM_LIMIT),
        name="mla_proj",
    )(x2, pos2, invf, g, wd, qg, kvg, wuq, wuk, wuvt, ones_col)


def _attention_units(units, kq_fn, vt_fn, st_ref, n_val, seq, finish_fn):
    kc = KEY_CHUNK
    nchunk = seq // kc

    def scores_chunk(ui, c, m):
        k, q = kq_fn(units[ui], c)
        st = lax.dot_general(k, q, (((1,), (1,)), ((), ())), preferred_element_type=F32)
        st_ref[ui % 2, c * kc:(c + 1) * kc, :] = st
        mc = jnp.max(st, axis=0, keepdims=True)
        return mc if m is None else jnp.maximum(m, mc)

    m_next = None
    for c in range(nchunk):
        m_next = scores_chunk(0, c, m_next)
    for ui in range(len(units)):
        m_cur, m_next, r, pt_prev = m_next, None, None, None
        for c in range(nchunk + 1):
            if ui + 1 < len(units) and c < nchunk:
                m_next = scores_chunk(ui + 1, c, m_next)
            if pt_prev is not None:
                rc = jnp.dot(vt_fn(units[ui], c - 1), pt_prev, preferred_element_type=F32)
                r = rc if r is None else r + rc
            if c < nchunk:
                pt_prev = jnp.exp2(st_ref[ui % 2, c * kc:(c + 1) * kc, :] - m_cur).astype(BF16)
        finish_fn(units[ui], r[:n_val] / r[n_val:n_val + 1])


def _mla_attn_kernel(q_ref, k_ref, vt_ref, o_ref, st_ref):
    seq = q_ref.shape[0]
    tq = st_ref.shape[2]
    rows = MLA_V + VT_ONES_ROWS
    kc = KEY_CHUNK
    nh = MLA_HEADS_PER_STEP
    units = [(hh, qi) for qi in range(seq // tq) for hh in range(nh)]
    done = {}

    def kq_fn(unit, c):
        hh, qi = unit
        blk = slice(hh * LANES, (hh + 1) * LANES)
        return k_ref[c * kc:(c + 1) * kc, blk], q_ref[qi * tq:(qi + 1) * tq, blk]

    def vt_fn(unit, c):
        hh, _ = unit
        return vt_ref[hh * rows:(hh + 1) * rows, c * kc:(c + 1) * kc]

    def finish_fn(unit, ot):
        hh, qi = unit
        done[hh] = ot
        if hh == nh - 1:
            o_ref[qi * tq:(qi + 1) * tq, :] = jnp.concatenate(
                [done[i] for i in range(nh)], axis=0).T.astype(BF16)

    _attention_units(units, kq_fn, vt_fn, st_ref, MLA_V, seq, finish_fn)


def _mla_attn(q, k, vt, batch, seq):
    t = q.shape[0]
    nh = MLA_HEADS_PER_STEP
    rows = nh * (MLA_V + VT_ONES_ROWS)
    return pl.pallas_call(
        _mla_attn_kernel,
        grid=(batch, MLA_HEADS // nh),
        in_specs=[pl.BlockSpec((seq, nh * LANES), lambda b, h: (b, h)),
                  pl.BlockSpec((seq, nh * LANES), lambda b, h: (b, h)),
                  pl.BlockSpec((rows, seq), lambda b, h: (h, b))],
        out_specs=pl.BlockSpec((seq, nh * MLA_V), lambda b, h: (b, h)),
        out_shape=jax.ShapeDtypeStruct((t, MLA_HEADS * MLA_V), BF16),
        scratch_shapes=[pltpu.VMEM((2, seq, Q_TILE), F32)],
        compiler_params=pltpu.CompilerParams(
            dimension_semantics=("parallel", "parallel"), vmem_limit_bytes=VMEM_LIMIT),
        name="mla_attn",
    )(q, k, vt)


def _diff_proj_kernel(x_ref, pos_ref, invf_ref, g_ref, w_ref, wvt_ref, ones_ref, q_ref, k_ref,
                      vt_ref):
    h = (_rms(x_ref[...]) * g_ref[...]).astype(BF16)
    qkv = jnp.dot(h, w_ref[...], preferred_element_type=F32)
    vt = lax.dot_general(wvt_ref[...], h, (((1,), (1,)), ((), ())), preferred_element_type=F32)
    vt_ref[...] = (vt + ones_ref[...]).astype(BF16)

    half = DIFF_ROPE // 2
    rest = DIFF_HEAD_DIM - DIFF_ROPE
    cos, sin = _rope_lane_tables(pos_ref, invf_ref, [True, True, rest, True, True, rest])
    lane = lax.broadcasted_iota(jnp.int32, cos.shape, 1)
    sin_lo = jnp.where(lane % DIFF_HEAD_DIM < half, -sin, 0.0)
    sin_hi = jnp.where(lane % DIFF_HEAD_DIM >= half, sin, 0.0)
    qs = (DIFF_HEAD_DIM ** -0.5) * LOG2E
    cos_q, sin_lo_q, sin_hi_q = cos * qs, sin_lo * qs, sin_hi * qs
    for hd in range(DIFF_HEADS):
        blk = slice(hd * LANES, (hd + 1) * LANES)
        kblk = slice(DIFF_QK_WIDTH + hd * LANES, DIFF_QK_WIDTH + (hd + 1) * LANES)
        q_ref[:, blk] = _apply_rope(qkv[:, blk], cos_q, sin_lo_q, sin_hi_q, half).astype(BF16)
        k_ref[:, blk] = _apply_rope(qkv[:, kblk], cos, sin_lo, sin_hi, half).astype(BF16)


def _diff_proj(x2, pos2, invf, g, w, wvt, ones_col):
    t = x2.shape[0]
    tm = TOKEN_TILE
    vt_rows = wvt.shape[0]
    full = lambda a: pl.BlockSpec(a.shape, lambda i: (0, 0))
    row = lambda n: pl.BlockSpec((tm, n), lambda i: (i, 0))
    return pl.pallas_call(
        _diff_proj_kernel,
        grid=(t // tm,),
        in_specs=[row(D_MODEL), pl.BlockSpec((None, 1, tm), lambda i: (i, 0, 0)), full(invf),
                  full(g), full(w), full(wvt), full(ones_col)],
        out_specs=[row(DIFF_QK_WIDTH), row(DIFF_QK_WIDTH),
                   pl.BlockSpec((vt_rows, tm), lambda i: (0, i))],
        out_shape=[jax.ShapeDtypeStruct((t, DIFF_QK_WIDTH), BF16),
                   jax.ShapeDtypeStruct((t, DIFF_QK_WIDTH), BF16),
                   jax.ShapeDtypeStruct((vt_rows, t), BF16)],
        compiler_params=pltpu.CompilerParams(dimension_semantics=("parallel",),
                                             vmem_limit_bytes=VMEM_LIMIT),
        name="diff_proj",
    )(x2, pos2, invf, g, w, wvt, ones_col)


def _diff_attn_kernel(lam_ref, q_ref, k_ref, vt_ref, o_ref, st_ref, *, lam_init):
    seq = q_ref.shape[0]
    tq = st_ref.shape[2]
    kc = KEY_CHUNK
    nh = DIFF_HEADS_PER_STEP
    rows = DIFF_V + VT_ONES_ROWS
    lam_p = lam_ref[...]
    lam = (jnp.exp(jnp.sum(lam_p[0:1] * lam_p[1:2], axis=-1, keepdims=True))
           - jnp.exp(jnp.sum(lam_p[2:3] * lam_p[3:4], axis=-1, keepdims=True)) + lam_init)
    lane = lax.broadcasted_iota(jnp.int32, (tq, LANES), 1)
    units = [(hh, comp, qi) for qi in range(seq // tq) for hh in range(nh) for comp in range(2)]
    masked_q = {}
    comps = {}
    done = {}

    def kq_fn(unit, c):
        hh, comp, qi = unit
        blk = slice(hh * LANES, (hh + 1) * LANES)
        if unit not in masked_q:
            q = q_ref[qi * tq:(qi + 1) * tq, blk]
            in_comp = (lane < DIFF_HEAD_DIM) if comp == 0 else (lane >= DIFF_HEAD_DIM)
            masked_q[unit] = jnp.where(in_comp, q, jnp.zeros_like(q))
        return k_ref[c * kc:(c + 1) * kc, blk], masked_q[unit]

    def vt_fn(unit, c):
        hh = unit[0]
        return vt_ref[hh * rows:(hh + 1) * rows, c * kc:(c + 1) * kc]

    def finish_fn(unit, ot):
        hh, comp, qi = unit
        comps[comp] = ot
        if comp == 1:
            o = comps[0] - lam * comps[1]
            done[hh] = (o * lax.rsqrt(jnp.mean(o * o, axis=0, keepdims=True) + NORM_EPS)
                        * (1.0 - lam_init))
            if hh == nh - 1:
                o_ref[qi * tq:(qi + 1) * tq, :] = jnp.concatenate(
                    [done[i] for i in range(nh)], axis=0).T.astype(BF16)

    _attention_units(units, kq_fn, vt_fn, st_ref, DIFF_V, seq, finish_fn)


def _diff_attn(lam_params, q, k, vt, batch, seq, lam_init):
    t = q.shape[0]
    nh = DIFF_HEADS_PER_STEP
    return pl.pallas_call(
        functools.partial(_diff_attn_kernel, lam_init=lam_init),
        grid=(batch, DIFF_HEADS // nh),
        in_specs=[pl.BlockSpec(lam_params.shape, lambda b, h: (0, 0)),
                  pl.BlockSpec((seq, nh * LANES), lambda b, h: (b, h)),
                  pl.BlockSpec((seq, nh * LANES), lambda b, h: (b, h)),
                  pl.BlockSpec((nh * (DIFF_V + VT_ONES_ROWS), seq), lambda b, h: (h, b))],
        out_specs=pl.BlockSpec((seq, nh * LANES), lambda b, h: (b, h)),
        out_shape=jax.ShapeDtypeStruct((t, DIFF_V_WIDTH), BF16),
        scratch_shapes=[pltpu.VMEM((2, seq, Q_TILE), F32)],
        compiler_params=pltpu.CompilerParams(
            dimension_semantics=("parallel", "parallel"), vmem_limit_bytes=VMEM_LIMIT),
        name="diff_attn",
    )(lam_params, q, k, vt)


def _ffn_kernel(o_ref, oprev_ref, onext_ref, x_ref, xprev_ref, xnext_ref, wo_ref, gf_ref, wg_ref,
                wv_ref, cwg_ref, cwv_ref, cbg_ref, cbv_ref, wdn_ref, fg_ref, out_ref, hext_ref,
                u_ref, xn_ref, acc_ref, *, nf, tiles_per_seq, apply_final_norm):
    s = pl.program_id(0)
    n_pairs = pl.num_programs(0) - 1
    up_pair = jnp.minimum(s, n_pairs - 1)
    up_tile, up_chunk = up_pair // nf, up_pair % nf
    chunk = jnp.maximum(s - 1, 0) % nf
    tm = x_ref.shape[0]
    hb = BF16_SUBLANES

    @pl.when(s == 0)
    def _():
        u_ref[1] = jnp.zeros(u_ref.shape[1:], F32)
        acc_ref[...] = jnp.zeros_like(acc_ref)

    @pl.when((up_chunk == 0) & (s < n_pairs))
    def _():
        at_start = (up_tile % tiles_per_seq) == 0
        at_end = (up_tile % tiles_per_seq) == tiles_per_seq - 1
        o_ext = jnp.concatenate([oprev_ref[...], o_ref[...], onext_ref[...]], axis=0)
        d = jnp.dot(o_ext, wo_ref[...], preferred_element_type=F32)
        g = gf_ref[...]
        xn = x_ref[...] + d[hb:hb + tm]
        xn_ref[...] = xn
        hext_ref[hb:hb + tm, :] = (_rms(xn) * g).astype(BF16)
        hprev = (_rms(xprev_ref[...] + d[:hb]) * g).astype(BF16)
        hnext = (_rms(xnext_ref[...] + d[hb + tm:]) * g).astype(BF16)
        hext_ref[0:hb, :] = jnp.where(at_start, jnp.zeros_like(hprev), hprev)
        hext_ref[hb + tm:, :] = jnp.where(at_end, jnp.zeros_like(hnext), hnext)

    @pl.when((s > 0) & (chunk == 0))
    def _():
        acc_ref[...] = xn_ref[...]

    prev = (s + 1) % 2

    def conv(j, cw_ref, cb_ref):
        cols = slice((j % 2) * LANES, (j % 2 + 1) * LANES)
        cw = cw_ref[...]
        return (u_ref[prev, j, hb - 1:hb - 1 + tm, :] * cw[0:1, cols]
                + u_ref[prev, j, hb:hb + tm, :] * cw[1:2, cols]
                + u_ref[prev, j, hb + 1:hb + 1 + tm, :] * cw[2:3, cols] + cb_ref[:, cols])

    gate = jnp.concatenate([conv(0, cwg_ref, cbg_ref), conv(1, cwg_ref, cbg_ref)], axis=1)
    val = jnp.concatenate([conv(2, cwv_ref, cbv_ref), conv(3, cwv_ref, cbv_ref)], axis=1)
    act = (gate * jax.nn.sigmoid(gate) * val).astype(BF16)
    acc_ref[...] += jnp.dot(act, wdn_ref[...], preferred_element_type=F32)

    hx = hext_ref[...]
    ug = jnp.dot(hx, wg_ref[...], preferred_element_type=F32)
    uv = jnp.dot(hx, wv_ref[...], preferred_element_type=F32)
    for j in range(2):
        u_ref[s % 2, j] = ug[:, j * LANES:(j + 1) * LANES]
        u_ref[s % 2, 2 + j] = uv[:, j * LANES:(j + 1) * LANES]

    @pl.when((s > 0) & (chunk == nf - 1))
    def _():
        y = acc_ref[...]
        if apply_final_norm:
            y = _rms(y) * fg_ref[...]
        out_ref[...] = y


def _ffn(o, x2, wo, gf, wup, cw, cb, wdn, fg, seq, apply_final_norm):
    t = x2.shape[0]
    tm = FFN_TILE
    fc = FF_CHUNK
    hb = BF16_SUBLANES
    nf = D_FF // fc
    n_pairs = (t // tm) * nf
    halo_per_tile = tm // hb
    last_halo = t // hb - 1
    up_tile = lambda s: jnp.minimum(s, n_pairs - 1) // nf
    up_chunk = lambda s: jnp.minimum(s, n_pairs - 1) % nf
    tile = lambda s: jnp.maximum(s - 1, 0) // nf
    chunk = lambda s: jnp.maximum(s - 1, 0) % nf
    main = pl.BlockSpec((tm, D_MODEL), lambda s: (up_tile(s), 0))
    prev_halo = pl.BlockSpec((hb, D_MODEL),
                             lambda s: (jnp.maximum(up_tile(s) * halo_per_tile - 1, 0), 0))
    next_halo = pl.BlockSpec((hb, D_MODEL),
                             lambda s: (jnp.minimum((up_tile(s) + 1) * halo_per_tile, last_halo), 0))
    return pl.pallas_call(
        functools.partial(_ffn_kernel, nf=nf, tiles_per_seq=seq // tm,
                          apply_final_norm=apply_final_norm),
        grid=(n_pairs + 1,),
        in_specs=[main, prev_halo, next_halo, main, prev_halo, next_halo,
                  pl.BlockSpec((D_MODEL, D_MODEL), lambda s: (0, 0)),
                  pl.BlockSpec((1, D_MODEL), lambda s: (0, 0)),
                  pl.BlockSpec((D_MODEL, fc), lambda s: (0, up_chunk(s))),
                  pl.BlockSpec((D_MODEL, fc), lambda s: (0, nf + up_chunk(s))),
                  pl.BlockSpec((CONV_WIDTH, fc), lambda s: (0, chunk(s))),
                  pl.BlockSpec((CONV_WIDTH, fc), lambda s: (0, nf + chunk(s))),
                  pl.BlockSpec((1, fc), lambda s: (0, chunk(s))),
                  pl.BlockSpec((1, fc), lambda s: (0, nf + chunk(s))),
                  pl.BlockSpec((fc, D_MODEL), lambda s: (chunk(s), 0)),
                  pl.BlockSpec((1, D_MODEL), lambda s: (0, 0))],
        out_specs=pl.BlockSpec((tm, D_MODEL), lambda s: (tile(s), 0)),
        out_shape=jax.ShapeDtypeStruct((t, D_MODEL), F32),
        scratch_shapes=[pltpu.VMEM((tm + 2 * hb, D_MODEL), BF16),
                        pltpu.VMEM((2, 4, tm + 2 * hb, LANES), F32),
                        pltpu.VMEM((tm, D_MODEL), F32),
                        pltpu.VMEM((tm, D_MODEL), F32)],
        compiler_params=pltpu.CompilerParams(dimension_semantics=("arbitrary",),
                                             vmem_limit_bytes=FFN_VMEM_LIMIT),
        name="conv_ffn",
    )(o, o, o, x2, x2, x2, wo, gf, wup, wup, cw, cw, cb, cb, wdn, fg)


def _inv_freq_col(rot_dim):
    return (ROPE_THETA ** (-jnp.arange(0, rot_dim, 2, dtype=F32) / rot_dim))[:, None]


def _pad_heads(w, heads, width):
    w = w.reshape(w.shape[0], heads, width)
    return jnp.pad(w, ((0, 0), (0, 0), (0, LANES - width))).reshape(w.shape[0], heads * LANES)


def _value_weight_t(w, heads, width):
    wt = w.T.reshape(heads, width, w.shape[0])
    wt = jnp.pad(wt, ((0, 0), (0, VT_ONES_ROWS), (0, 0))).reshape(-1, w.shape[0])
    ones = jnp.pad(jnp.zeros((heads, width, 1), F32), ((0, 0), (0, VT_ONES_ROWS), (0, 0)),
                   constant_values=1.0).reshape(-1, 1)
    return wt.astype(BF16), ones


def kernel(x, positions, attn_norm_g, ffn_norm_g, final_norm_g, mla_w_down, mla_q_norm_g,
           mla_kv_norm_g, mla_w_uq, mla_w_ukv, mla_w_o, diff_w_qkv, diff_lam_q1, diff_lam_k1,
           diff_lam_q2, diff_lam_k2, diff_w_o, ffn_w_up, ffn_conv_w, ffn_conv_b, ffn_w_down):
    batch, seq, _ = x.shape
    t = batch * seq
    x2 = x.reshape(t, D_MODEL)
    pos2 = positions.reshape(t // TOKEN_TILE, 1, TOKEN_TILE)
    row = lambda a: a.reshape(1, -1)

    def out_proj_ffn(o, x2, w_o, layer, final):
        return _ffn(o, x2, w_o.astype(BF16), row(ffn_norm_g[layer]),
                    ffn_w_up[layer].astype(BF16), ffn_conv_w[layer], row(ffn_conv_b[layer]),
                    ffn_w_down[layer].astype(BF16), row(final_norm_g), seq, final)

    wd = mla_w_down[0]
    k_rope_cols = jnp.pad(wd[:, MLA_Q_LORA + MLA_KV_LORA:], ((0, 0), (MLA_NOPE, LANES - MLA_QK)))
    wd_p = jnp.concatenate([wd[:, :MLA_Q_LORA + MLA_KV_LORA], k_rope_cols], axis=1).astype(BF16)
    wukv = mla_w_ukv[0].reshape(MLA_KV_LORA, MLA_HEADS, MLA_NOPE + MLA_V)
    wuk_p = _pad_heads(wukv[:, :, :MLA_NOPE].reshape(MLA_KV_LORA, -1), MLA_HEADS, MLA_NOPE)
    wuvt, ones_col = _value_weight_t(wukv[:, :, MLA_NOPE:].reshape(MLA_KV_LORA, -1), MLA_HEADS,
                                     MLA_V)
    q, k, vt = _mla_proj(x2, pos2, _inv_freq_col(MLA_ROPE), row(attn_norm_g[0]),
                         wd_p, row(mla_q_norm_g[0]), row(mla_kv_norm_g[0]),
                         _pad_heads(mla_w_uq[0], MLA_HEADS, MLA_QK).astype(BF16),
                         wuk_p.astype(BF16), wuvt, ones_col)
    o = _mla_attn(q, k, vt, batch, seq)
    x2 = out_proj_ffn(o, x2, mla_w_o[0], 0, False)

    lam_init = 0.8 - 0.6 * math.exp(-0.3 * 1)
    wvt, ones_col = _value_weight_t(diff_w_qkv[0][:, 2 * DIFF_QK_WIDTH:], DIFF_HEADS, DIFF_V)
    q, k, vt = _diff_proj(x2, pos2, _inv_freq_col(DIFF_ROPE),
                          row(attn_norm_g[1]), diff_w_qkv[0][:, :2 * DIFF_QK_WIDTH].astype(BF16),
                          wvt, ones_col)
    lam_params = jnp.stack([diff_lam_q1[0], diff_lam_k1[0], diff_lam_q2[0], diff_lam_k2[0]])
    o = _diff_attn(lam_params, q, k, vt, batch, seq, lam_init)
    x2 = out_proj_ffn(o, x2, diff_w_o[0], 1, True)
    return x2.reshape(batch, seq, D_MODEL)
```

```python
import functools
import math

import jax
import jax.numpy as jnp
from jax import lax
from jax.experimental import pallas as pl
from jax.experimental.pallas import tpu as pltpu

F32 = jnp.float32
BF16 = jnp.bfloat16

D_MODEL = 1024
ROPE_THETA = 500000.0
NORM_EPS = 1e-6

MLA_HEADS = 16
MLA_Q_LORA = 384
MLA_KV_LORA = 256
MLA_NOPE = 64
MLA_ROPE = 32
MLA_V = 64
MLA_QK = MLA_NOPE + MLA_ROPE

DIFF_HEADS = 8
DIFF_HEAD_DIM = 64
DIFF_V = 2 * DIFF_HEAD_DIM
DIFF_ROPE = DIFF_HEAD_DIM // 4
DIFF_QK_WIDTH = DIFF_HEADS * 2 * DIFF_HEAD_DIM
DIFF_V_WIDTH = DIFF_HEADS * DIFF_V

D_FF = 2816
CONV_WIDTH = 3

LANES = 128
BF16_SUBLANES = 16
VT_ONES_ROWS = BF16_SUBLANES
LOG2E = math.log2(math.e)

TOKEN_TILE = 512
FFN_TILE = 1024
Q_TILE = 2048
KEY_CHUNK = 256
MLA_HEADS_PER_STEP = 4
DIFF_HEADS_PER_STEP = 4
FF_CHUNK = 256
VMEM_LIMIT = 48 * 1024 * 1024
LARGE_VMEM_LIMIT = 56 * 1024 * 1024


def _rms(x):
    return x * lax.rsqrt(jnp.mean(x * x, axis=-1, keepdims=True) + NORM_EPS)


def _apply_rope(x, cos, sin_lo, sin_hi, half):
    return (x * cos + pltpu.roll(x, LANES - half, 1) * sin_lo
            + pltpu.roll(x, half, 1) * sin_hi)


def _rope_lane_tables(pos_ref, invf_ref, pieces):
    ang = invf_ref[...] * pos_ref[...].astype(F32)
    cos, sin = jnp.cos(ang), jnp.sin(ang)
    tm = ang.shape[1]
    cos_rows = [cos if p is True else jnp.ones((p, tm), F32) for p in pieces]
    sin_rows = [sin if p is True else jnp.zeros((p, tm), F32) for p in pieces]
    return jnp.concatenate(cos_rows, axis=0).T, jnp.concatenate(sin_rows, axis=0).T


def _mla_proj_kernel(x_ref, pos_ref, invf_ref, g_ref, wd_ref, qg_ref, kvg_ref, wuq_ref, wuk_ref,
                     wuvt_ref, ones_ref, q_ref, k_ref, vt_ref):
    h = _rms(x_ref[...]) * g_ref[...]
    down = jnp.dot(h.astype(BF16), wd_ref[...], preferred_element_type=F32)
    c_q = down[:, :MLA_Q_LORA]
    c_kv = down[:, MLA_Q_LORA:MLA_Q_LORA + MLA_KV_LORA]
    k_rope = down[:, MLA_Q_LORA + MLA_KV_LORA:]
    cqn = (_rms(c_q) * qg_ref[...]).astype(BF16)
    ckvn = (_rms(c_kv) * kvg_ref[...]).astype(BF16)
    q = jnp.dot(cqn, wuq_ref[...], preferred_element_type=F32)
    kn = jnp.dot(ckvn, wuk_ref[...], preferred_element_type=F32)
    vt = lax.dot_general(wuvt_ref[...], ckvn, (((1,), (1,)), ((), ())), preferred_element_type=F32)
    vt_ref[...] = (vt + ones_ref[...]).astype(BF16)

    half = MLA_ROPE // 2
    cos, sin = _rope_lane_tables(pos_ref, invf_ref, [MLA_NOPE, True, True, LANES - MLA_QK])
    lane = lax.broadcasted_iota(jnp.int32, cos.shape, 1)
    sin_lo = jnp.where(lane < MLA_NOPE + half, -sin, 0.0)
    sin_hi = jnp.where(lane >= MLA_NOPE + half, sin, 0.0)
    kr = _apply_rope(k_rope, cos, sin_lo, sin_hi, half)
    qs = (MLA_QK ** -0.5) * LOG2E
    cos_q, sin_lo_q, sin_hi_q = cos * qs, sin_lo * qs, sin_hi * qs
    for hd in range(MLA_HEADS):
        blk = slice(hd * LANES, (hd + 1) * LANES)
        q_ref[:, blk] = _apply_rope(q[:, blk], cos_q, sin_lo_q, sin_hi_q, half).astype(BF16)
        k_ref[:, blk] = (kn[:, blk] + kr).astype(BF16)


def _mla_proj(x2, pos2, invf, g, wd, qg, kvg, wuq, wuk, wuvt, ones_col):
    t = x2.shape[0]
    tm = TOKEN_TILE
    full = lambda a: pl.BlockSpec(a.shape, lambda i: (0, 0))
    vt_rows = wuvt.shape[0]
    return pl.pallas_call(
        _mla_proj_kernel,
        grid=(t // tm,),
        in_specs=[pl.BlockSpec((tm, D_MODEL), lambda i: (i, 0)),
                  pl.BlockSpec((None, 1, tm), lambda i: (i, 0, 0)),
                  full(invf), full(g), full(wd), full(qg), full(kvg), full(wuq), full(wuk),
                  full(wuvt), full(ones_col)],
        out_specs=[pl.BlockSpec((tm, MLA_HEADS * LANES), lambda i: (i, 0)),
                   pl.BlockSpec((tm, MLA_HEADS * LANES), lambda i: (i, 0)),
                   pl.BlockSpec((vt_rows, tm), lambda i: (0, i))],
        out_shape=[jax.ShapeDtypeStruct((t, MLA_HEADS * LANES), BF16),
                   jax.ShapeDtypeStruct((t, MLA_HEADS * LANES), BF16),
                   jax.ShapeDtypeStruct((vt_rows, t), BF16)],
        compiler_params=pltpu.CompilerParams(dimension_semantics=("parallel",),
                                             vmem_limit_bytes=VMEM_LIMIT),
        name="mla_proj",
    )(x2, pos2, invf, g, wd, qg, kvg, wuq, wuk, wuvt, ones_col)


def _attention_units(units, kq_fn, vt_fn, st_ref, n_val, seq, finish_fn):
    kc = KEY_CHUNK
    nchunk = seq // kc

    def scores_chunk(ui, c, m):
        k, q = kq_fn(units[ui], c)
        st = lax.dot_general(k, q, (((1,), (1,)), ((), ())), preferred_element_type=F32)
        st_ref[ui % 2, c * kc:(c + 1) * kc, :] = st
        mc = jnp.max(st, axis=0, keepdims=True)
        return mc if m is None else jnp.maximum(m, mc)

    m_next = None
    for c in range(nchunk):
        m_next = scores_chunk(0, c, m_next)
    for ui in range(len(units)):
        m_cur, m_next, r, pt_prev = m_next, None, None, None
        for c in range(nchunk + 1):
            if ui + 1 < len(units) and c < nchunk:
                m_next = scores_chunk(ui + 1, c, m_next)
            if pt_prev is not None:
                rc = jnp.dot(vt_fn(units[ui], c - 1), pt_prev, preferred_element_type=F32)
                r = rc if r is None else r + rc
            if c < nchunk:
                pt_prev = jnp.exp2(st_ref[ui % 2, c * kc:(c + 1) * kc, :] - m_cur).astype(BF16)
        finish_fn(units[ui], r[:n_val] / r[n_val:n_val + 1])


def _mla_attn_kernel(q_ref, k_ref, vt_ref, o_ref, st_ref):
    seq = q_ref.shape[0]
    tq = st_ref.shape[2]
    rows = MLA_V + VT_ONES_ROWS
    kc = KEY_CHUNK
    nh = MLA_HEADS_PER_STEP
    units = [(hh, qi) for qi in range(seq // tq) for hh in range(nh)]
    done = {}

    def kq_fn(unit, c):
        hh, qi = unit
        blk = slice(hh * LANES, (hh + 1) * LANES)
        return k_ref[c * kc:(c + 1) * kc, blk], q_ref[qi * tq:(qi + 1) * tq, blk]

    def vt_fn(unit, c):
        hh, _ = unit
        return vt_ref[hh * rows:(hh + 1) * rows, c * kc:(c + 1) * kc]

    def finish_fn(unit, ot):
        hh, qi = unit
        done[hh] = ot
        if hh == nh - 1:
            o_ref[qi * tq:(qi + 1) * tq, :] = jnp.concatenate(
                [done[i] for i in range(nh)], axis=0).T.astype(BF16)

    _attention_units(units, kq_fn, vt_fn, st_ref, MLA_V, seq, finish_fn)


def _mla_attn(q, k, vt, batch, seq):
    t = q.shape[0]
    nh = MLA_HEADS_PER_STEP
    rows = nh * (MLA_V + VT_ONES_ROWS)
    return pl.pallas_call(
        _mla_attn_kernel,
        grid=(batch, MLA_HEADS // nh),
        in_specs=[pl.BlockSpec((seq, nh * LANES), lambda b, h: (b, h)),
                  pl.BlockSpec((seq, nh * LANES), lambda b, h: (b, h)),
                  pl.BlockSpec((rows, seq), lambda b, h: (h, b))],
        out_specs=pl.BlockSpec((seq, nh * MLA_V), lambda b, h: (b, h)),
        out_shape=jax.ShapeDtypeStruct((t, MLA_HEADS * MLA_V), BF16),
        scratch_shapes=[pltpu.VMEM((2, seq, Q_TILE), F32)],
        compiler_params=pltpu.CompilerParams(
            dimension_semantics=("parallel", "parallel"), vmem_limit_bytes=VMEM_LIMIT),
        name="mla_attn",
    )(q, k, vt)


def _diff_proj_kernel(x_ref, pos_ref, invf_ref, g_ref, w_ref, wvt_ref, ones_ref, q_ref, k_ref,
                      vt_ref):
    h = (_rms(x_ref[...]) * g_ref[...]).astype(BF16)
    qkv = jnp.dot(h, w_ref[...], preferred_element_type=F32)
    vt = lax.dot_general(wvt_ref[...], h, (((1,), (1,)), ((), ())), preferred_element_type=F32)
    vt_ref[...] = (vt + ones_ref[...]).astype(BF16)

    half = DIFF_ROPE // 2
    rest = DIFF_HEAD_DIM - DIFF_ROPE
    cos, sin = _rope_lane_tables(pos_ref, invf_ref, [True, True, rest, True, True, rest])
    lane = lax.broadcasted_iota(jnp.int32, cos.shape, 1)
    sin_lo = jnp.where(lane % DIFF_HEAD_DIM < half, -sin, 0.0)
    sin_hi = jnp.where(lane % DIFF_HEAD_DIM >= half, sin, 0.0)
    qs = (DIFF_HEAD_DIM ** -0.5) * LOG2E
    cos_q, sin_lo_q, sin_hi_q = cos * qs, sin_lo * qs, sin_hi * qs
    for hd in range(DIFF_HEADS):
        blk = slice(hd * LANES, (hd + 1) * LANES)
        kblk = slice(DIFF_QK_WIDTH + hd * LANES, DIFF_QK_WIDTH + (hd + 1) * LANES)
        q_ref[:, blk] = _apply_rope(qkv[:, blk], cos_q, sin_lo_q, sin_hi_q, half).astype(BF16)
        k_ref[:, blk] = _apply_rope(qkv[:, kblk], cos, sin_lo, sin_hi, half).astype(BF16)


def _diff_proj(x2, pos2, invf, g, w, wvt, ones_col):
    t = x2.shape[0]
    tm = TOKEN_TILE
    vt_rows = wvt.shape[0]
    full = lambda a: pl.BlockSpec(a.shape, lambda i: (0, 0))
    row = lambda n: pl.BlockSpec((tm, n), lambda i: (i, 0))
    return pl.pallas_call(
        _diff_proj_kernel,
        grid=(t // tm,),
        in_specs=[row(D_MODEL), pl.BlockSpec((None, 1, tm), lambda i: (i, 0, 0)), full(invf),
                  full(g), full(w), full(wvt), full(ones_col)],
        out_specs=[row(DIFF_QK_WIDTH), row(DIFF_QK_WIDTH),
                   pl.BlockSpec((vt_rows, tm), lambda i: (0, i))],
        out_shape=[jax.ShapeDtypeStruct((t, DIFF_QK_WIDTH), BF16),
                   jax.ShapeDtypeStruct((t, DIFF_QK_WIDTH), BF16),
                   jax.ShapeDtypeStruct((vt_rows, t), BF16)],
        compiler_params=pltpu.CompilerParams(dimension_semantics=("parallel",),
                                             vmem_limit_bytes=VMEM_LIMIT),
        name="diff_proj",
    )(x2, pos2, invf, g, w, wvt, ones_col)


def _diff_attn_kernel(lam_ref, q_ref, k_ref, vt_ref, o_ref, st_ref, *, lam_init):
    seq = q_ref.shape[0]
    tq = st_ref.shape[2]
    kc = KEY_CHUNK
    nh = DIFF_HEADS_PER_STEP
    rows = DIFF_V + VT_ONES_ROWS
    lam_p = lam_ref[...]
    lam = (jnp.exp(jnp.sum(lam_p[0:1] * lam_p[1:2], axis=-1, keepdims=True))
           - jnp.exp(jnp.sum(lam_p[2:3] * lam_p[3:4], axis=-1, keepdims=True)) + lam_init)
    lane = lax.broadcasted_iota(jnp.int32, (tq, LANES), 1)
    units = [(hh, comp, qi) for qi in range(seq // tq) for hh in range(nh) for comp in range(2)]
    masked_q = {}
    comps = {}
    done = {}

    def kq_fn(unit, c):
        hh, comp, qi = unit
        blk = slice(hh * LANES, (hh + 1) * LANES)
        if unit not in masked_q:
            q = q_ref[qi * tq:(qi + 1) * tq, blk]
            in_comp = (lane < DIFF_HEAD_DIM) if comp == 0 else (lane >= DIFF_HEAD_DIM)
            masked_q[unit] = jnp.where(in_comp, q, jnp.zeros_like(q))
        return k_ref[c * kc:(c + 1) * kc, blk], masked_q[unit]

    def vt_fn(unit, c):
        hh = unit[0]
        return vt_ref[hh * rows:(hh + 1) * rows, c * kc:(c + 1) * kc]

    def finish_fn(unit, ot):
        hh, comp, qi = unit
        comps[comp] = ot
        if comp == 1:
            o = comps[0] - lam * comps[1]
            done[hh] = (o * lax.rsqrt(jnp.mean(o * o, axis=0, keepdims=True) + NORM_EPS)
                        * (1.0 - lam_init))
            if hh == nh - 1:
                o_ref[qi * tq:(qi + 1) * tq, :] = jnp.concatenate(
                    [done[i] for i in range(nh)], axis=0).T.astype(BF16)

    _attention_units(units, kq_fn, vt_fn, st_ref, DIFF_V, seq, finish_fn)


def _diff_attn(lam_params, q, k, vt, batch, seq, lam_init):
    t = q.shape[0]
    nh = DIFF_HEADS_PER_STEP
    return pl.pallas_call(
        functools.partial(_diff_attn_kernel, lam_init=lam_init),
        grid=(batch, DIFF_HEADS // nh),
        in_specs=[pl.BlockSpec(lam_params.shape, lambda b, h: (0, 0)),
                  pl.BlockSpec((seq, nh * LANES), lambda b, h: (b, h)),
                  pl.BlockSpec((seq, nh * LANES), lambda b, h: (b, h)),
                  pl.BlockSpec((nh * (DIFF_V + VT_ONES_ROWS), seq), lambda b, h: (h, b))],
        out_specs=pl.BlockSpec((seq, nh * LANES), lambda b, h: (b, h)),
        out_shape=jax.ShapeDtypeStruct((t, DIFF_V_WIDTH), BF16),
        scratch_shapes=[pltpu.VMEM((2, seq, Q_TILE), F32)],
        compiler_params=pltpu.CompilerParams(
            dimension_semantics=("parallel", "parallel"), vmem_limit_bytes=LARGE_VMEM_LIMIT),
        name="diff_attn",
    )(lam_params, q, k, vt)


def _ffn_kernel(o_ref, oprev_ref, onext_ref, x_ref, xprev_ref, xnext_ref, wo_ref, gf_ref, wg_ref,
                wv_ref, cwg_ref, cwv_ref, cbg_ref, cbv_ref, wdn_ref, fg_ref, out_ref, hext_ref,
                u_ref, xn_ref, acc_ref, *, nf, tiles_per_seq, apply_final_norm):
    s = pl.program_id(0)
    n_pairs = pl.num_programs(0) - 1
    up_pair = jnp.minimum(s, n_pairs - 1)
    up_tile, up_chunk = up_pair // nf, up_pair % nf
    chunk = jnp.maximum(s - 1, 0) % nf
    tm = x_ref.shape[0]
    hb = BF16_SUBLANES

    @pl.when(s == 0)
    def _():
        u_ref[1] = jnp.zeros(u_ref.shape[1:], F32)
        acc_ref[...] = jnp.zeros_like(acc_ref)

    @pl.when((up_chunk == 0) & (s < n_pairs))
    def _():
        at_start = (up_tile % tiles_per_seq) == 0
        at_end = (up_tile % tiles_per_seq) == tiles_per_seq - 1
        o_ext = jnp.concatenate([oprev_ref[...], o_ref[...], onext_ref[...]], axis=0)
        d = jnp.dot(o_ext, wo_ref[...], preferred_element_type=F32)
        g = gf_ref[...]
        xn = x_ref[...] + d[hb:hb + tm]
        xn_ref[...] = xn
        hext_ref[hb:hb + tm, :] = (_rms(xn) * g).astype(BF16)
        hprev = (_rms(xprev_ref[...] + d[:hb]) * g).astype(BF16)
        hnext = (_rms(xnext_ref[...] + d[hb + tm:]) * g).astype(BF16)
        hext_ref[0:hb, :] = jnp.where(at_start, jnp.zeros_like(hprev), hprev)
        hext_ref[hb + tm:, :] = jnp.where(at_end, jnp.zeros_like(hnext), hnext)

    @pl.when((s > 0) & (chunk == 0))
    def _():
        acc_ref[...] = xn_ref[...]

    prev = (s + 1) % 2

    def conv(j, cw_ref, cb_ref):
        cols = slice((j % 2) * LANES, (j % 2 + 1) * LANES)
        cw = cw_ref[...]
        return (u_ref[prev, j, hb - 1:hb - 1 + tm, :] * cw[0:1, cols]
                + u_ref[prev, j, hb:hb + tm, :] * cw[1:2, cols]
                + u_ref[prev, j, hb + 1:hb + 1 + tm, :] * cw[2:3, cols] + cb_ref[:, cols])

    gate = jnp.concatenate([conv(0, cwg_ref, cbg_ref), conv(1, cwg_ref, cbg_ref)], axis=1)
    val = jnp.concatenate([conv(2, cwv_ref, cbv_ref), conv(3, cwv_ref, cbv_ref)], axis=1)
    act = (gate * jax.nn.sigmoid(gate) * val).astype(BF16)
    acc_ref[...] += jnp.dot(act, wdn_ref[...], preferred_element_type=F32)

    hx = hext_ref[...]
    ug = jnp.dot(hx, wg_ref[...], preferred_element_type=F32)
    uv = jnp.dot(hx, wv_ref[...], preferred_element_type=F32)
    for j in range(2):
        u_ref[s % 2, j] = ug[:, j * LANES:(j + 1) * LANES]
        u_ref[s % 2, 2 + j] = uv[:, j * LANES:(j + 1) * LANES]

    @pl.when((s > 0) & (chunk == nf - 1))
    def _():
        y = acc_ref[...]
        if apply_final_norm:
            y = _rms(y) * fg_ref[...]
        out_ref[...] = y


def _ffn(o, x2, wo, gf, wup, cw, cb, wdn, fg, seq, apply_final_norm):
    t = x2.shape[0]
    tm = FFN_TILE
    fc = FF_CHUNK
    hb = BF16_SUBLANES
    nf = D_FF // fc
    n_pairs = (t // tm) * nf
    halo_per_tile = tm // hb
    last_halo = t // hb - 1
    up_tile = lambda s: jnp.minimum(s, n_pairs - 1) // nf
    up_chunk = lambda s: jnp.minimum(s, n_pairs - 1) % nf
    tile = lambda s: jnp.maximum(s - 1, 0) // nf
    chunk = lambda s: jnp.maximum(s - 1, 0) % nf
    main = pl.BlockSpec((tm, D_MODEL), lambda s: (up_tile(s), 0))
    prev_halo = pl.BlockSpec((hb, D_MODEL),
                             lambda s: (jnp.maximum(up_tile(s) * halo_per_tile - 1, 0), 0))
    next_halo = pl.BlockSpec((hb, D_MODEL),
                             lambda s: (jnp.minimum((up_tile(s) + 1) * halo_per_tile, last_halo), 0))
    return pl.pallas_call(
        functools.partial(_ffn_kernel, nf=nf, tiles_per_seq=seq // tm,
                          apply_final_norm=apply_final_norm),
        grid=(n_pairs + 1,),
        in_specs=[main, prev_halo, next_halo, main, prev_halo, next_halo,
                  pl.BlockSpec((D_MODEL, D_MODEL), lambda s: (0, 0)),
                  pl.BlockSpec((1, D_MODEL), lambda s: (0, 0)),
                  pl.BlockSpec((D_MODEL, fc), lambda s: (0, up_chunk(s))),
                  pl.BlockSpec((D_MODEL, fc), lambda s: (0, nf + up_chunk(s))),
                  pl.BlockSpec((CONV_WIDTH, fc), lambda s: (0, chunk(s))),
                  pl.BlockSpec((CONV_WIDTH, fc), lambda s: (0, nf + chunk(s))),
                  pl.BlockSpec((1, fc), lambda s: (0, chunk(s))),
                  pl.BlockSpec((1, fc), lambda s: (0, nf + chunk(s))),
                  pl.BlockSpec((fc, D_MODEL), lambda s: (chunk(s), 0)),
                  pl.BlockSpec((1, D_MODEL), lambda s: (0, 0))],
        out_specs=pl.BlockSpec((tm, D_MODEL), lambda s: (tile(s), 0)),
        out_shape=jax.ShapeDtypeStruct((t, D_MODEL), F32),
        scratch_shapes=[pltpu.VMEM((tm + 2 * hb, D_MODEL), BF16),
                        pltpu.VMEM((2, 4, tm + 2 * hb, LANES), F32),
                        pltpu.VMEM((tm, D_MODEL), F32),
                        pltpu.VMEM((tm, D_MODEL), F32)],
        compiler_params=pltpu.CompilerParams(dimension_semantics=("arbitrary",),
                                             vmem_limit_bytes=LARGE_VMEM_LIMIT),
        name="conv_ffn",
    )(o, o, o, x2, x2, x2, wo, gf, wup, wup, cw, cw, cb, cb, wdn, fg)


def _inv_freq_col(rot_dim):
    return (ROPE_THETA ** (-jnp.arange(0, rot_dim, 2, dtype=F32) / rot_dim))[:, None]


def _pad_heads(w, heads, width):
    w = w.reshape(w.shape[0], heads, width)
    return jnp.pad(w, ((0, 0), (0, 0), (0, LANES - width))).reshape(w.shape[0], heads * LANES)


def _value_weight_t(w, heads, width):
    wt = w.T.reshape(heads, width, w.shape[0])
    wt = jnp.pad(wt, ((0, 0), (0, VT_ONES_ROWS), (0, 0))).reshape(-1, w.shape[0])
    ones = jnp.pad(jnp.zeros((heads, width, 1), F32), ((0, 0), (0, VT_ONES_ROWS), (0, 0)),
                   constant_values=1.0).reshape(-1, 1)
    return wt.astype(BF16), ones


def kernel(x, positions, attn_norm_g, ffn_norm_g, final_norm_g, mla_w_down, mla_q_norm_g,
           mla_kv_norm_g, mla_w_uq, mla_w_ukv, mla_w_o, diff_w_qkv, diff_lam_q1, diff_lam_k1,
           diff_lam_q2, diff_lam_k2, diff_w_o, ffn_w_up, ffn_conv_w, ffn_conv_b, ffn_w_down):
    batch, seq, _ = x.shape
    t = batch * seq
    x2 = x.reshape(t, D_MODEL)
    pos2 = positions.reshape(t // TOKEN_TILE, 1, TOKEN_TILE)
    row = lambda a: a.reshape(1, -1)

    def out_proj_ffn(o, x2, w_o, layer, final):
        return _ffn(o, x2, w_o.astype(BF16), row(ffn_norm_g[layer]),
                    ffn_w_up[layer].astype(BF16), ffn_conv_w[layer], row(ffn_conv_b[layer]),
                    ffn_w_down[layer].astype(BF16), row(final_norm_g), seq, final)

    wd = mla_w_down[0]
    k_rope_cols = jnp.pad(wd[:, MLA_Q_LORA + MLA_KV_LORA:], ((0, 0), (MLA_NOPE, LANES - MLA_QK)))
    wd_p = jnp.concatenate([wd[:, :MLA_Q_LORA + MLA_KV_LORA], k_rope_cols], axis=1).astype(BF16)
    wukv = mla_w_ukv[0].reshape(MLA_KV_LORA, MLA_HEADS, MLA_NOPE + MLA_V)
    wuk_p = _pad_heads(wukv[:, :, :MLA_NOPE].reshape(MLA_KV_LORA, -1), MLA_HEADS, MLA_NOPE)
    wuvt, ones_col = _value_weight_t(wukv[:, :, MLA_NOPE:].reshape(MLA_KV_LORA, -1), MLA_HEADS,
                                     MLA_V)
    q, k, vt = _mla_proj(x2, pos2, _inv_freq_col(MLA_ROPE), row(attn_norm_g[0]),
                         wd_p, row(mla_q_norm_g[0]), row(mla_kv_norm_g[0]),
                         _pad_heads(mla_w_uq[0], MLA_HEADS, MLA_QK).astype(BF16),
                         wuk_p.astype(BF16), wuvt, ones_col)
    o = _mla_attn(q, k, vt, batch, seq)
    x2 = out_proj_ffn(o, x2, mla_w_o[0], 0, False)

    lam_init = 0.8 - 0.6 * math.exp(-0.3 * 1)
    wvt, ones_col = _value_weight_t(diff_w_qkv[0][:, 2 * DIFF_QK_WIDTH:], DIFF_HEADS, DIFF_V)
    q, k, vt = _diff_proj(x2, pos2, _inv_freq_col(DIFF_ROPE),
                          row(attn_norm_g[1]), diff_w_qkv[0][:, :2 * DIFF_QK_WIDTH].astype(BF16),
                          wvt, ones_col)
    lam_params = jnp.stack([diff_lam_q1[0], diff_lam_k1[0], diff_lam_q2[0], diff_lam_k2[0]])
    o = _diff_attn(lam_params, q, k, vt, batch, seq, lam_init)
    x2 = out_proj_ffn(o, x2, diff_w_o[0], 1, True)
    return x2.reshape(batch, seq, D_MODEL)
```

```python
import functools
import math

import jax
import jax.numpy as jnp
from jax import lax
from jax.experimental import pallas as pl
from jax.experimental.pallas import tpu as pltpu

F32 = jnp.float32
BF16 = jnp.bfloat16

D_MODEL = 1024
ROPE_THETA = 500000.0
NORM_EPS = 1e-6

MLA_HEADS = 16
MLA_Q_LORA = 384
MLA_KV_LORA = 256
MLA_NOPE = 64
MLA_ROPE = 32
MLA_V = 64
MLA_QK = MLA_NOPE + MLA_ROPE

DIFF_HEADS = 8
DIFF_HEAD_DIM = 64
DIFF_V = 2 * DIFF_HEAD_DIM
DIFF_ROPE = DIFF_HEAD_DIM // 4
DIFF_QK_WIDTH = DIFF_HEADS * 2 * DIFF_HEAD_DIM
DIFF_V_WIDTH = DIFF_HEADS * DIFF_V

D_FF = 2816
CONV_WIDTH = 3

LANES = 128
BF16_SUBLANES = 16
VT_ONES_ROWS = BF16_SUBLANES
LOG2E = math.log2(math.e)

TOKEN_TILE = 512
FFN_TILE = 1024
Q_TILE = 2048
KEY_CHUNK = 256
MLA_HEADS_PER_STEP = 4
DIFF_HEADS_PER_STEP = 2
FF_CHUNK = 256
VMEM_LIMIT = 48 * 1024 * 1024
FFN_VMEM_LIMIT = 56 * 1024 * 1024


def _rms(x):
    return x * lax.rsqrt(jnp.mean(x * x, axis=-1, keepdims=True) + NORM_EPS)


def _apply_rope(x, cos, sin_lo, sin_hi, half):
    return (x * cos + pltpu.roll(x, LANES - half, 1) * sin_lo
            + pltpu.roll(x, half, 1) * sin_hi)


def _rope_lane_tables(pos_ref, invf_ref, pieces):
    ang = invf_ref[...] * pos_ref[...].astype(F32)
    cos, sin = jnp.cos(ang), jnp.sin(ang)
    tm = ang.shape[1]
    cos_rows = [cos if p is True else jnp.ones((p, tm), F32) for p in pieces]
    sin_rows = [sin if p is True else jnp.zeros((p, tm), F32) for p in pieces]
    return jnp.concatenate(cos_rows, axis=0).T, jnp.concatenate(sin_rows, axis=0).T


def _mla_proj_kernel(x_ref, pos_ref, invf_ref, g_ref, wd_ref, qg_ref, kvg_ref, wuq_ref, wuk_ref,
                     wuvt_ref, ones_ref, q_ref, k_ref, vt_ref):
    h = _rms(x_ref[...]) * g_ref[...]
    down = jnp.dot(h.astype(BF16), wd_ref[...], preferred_element_type=F32)
    c_q = down[:, :MLA_Q_LORA]
    c_kv = down[:, MLA_Q_LORA:MLA_Q_LORA + MLA_KV_LORA]
    k_rope = down[:, MLA_Q_LORA + MLA_KV_LORA:]
    cqn = (_rms(c_q) * qg_ref[...]).astype(BF16)
    ckvn = (_rms(c_kv) * kvg_ref[...]).astype(BF16)
    q = jnp.dot(cqn, wuq_ref[...], preferred_element_type=F32)
    kn = jnp.dot(ckvn, wuk_ref[...], preferred_element_type=F32)
    vt = lax.dot_general(wuvt_ref[...], ckvn, (((1,), (1,)), ((), ())), preferred_element_type=F32)
    vt_ref[...] = (vt + ones_ref[...]).astype(BF16)

    half = MLA_ROPE // 2
    cos, sin = _rope_lane_tables(pos_ref, invf_ref, [MLA_NOPE, True, True, LANES - MLA_QK])
    lane = lax.broadcasted_iota(jnp.int32, cos.shape, 1)
    sin_lo = jnp.where(lane < MLA_NOPE + half, -sin, 0.0)
    sin_hi = jnp.where(lane >= MLA_NOPE + half, sin, 0.0)
    kr = _apply_rope(k_rope, cos, sin_lo, sin_hi, half)
    qs = (MLA_QK ** -0.5) * LOG2E
    cos_q, sin_lo_q, sin_hi_q = cos * qs, sin_lo * qs, sin_hi * qs
    for hd in range(MLA_HEADS):
        blk = slice(hd * LANES, (hd + 1) * LANES)
        q_ref[:, blk] = _apply_rope(q[:, blk], cos_q, sin_lo_q, sin_hi_q, half).astype(BF16)
        k_ref[:, blk] = (kn[:, blk] + kr).astype(BF16)


def _mla_proj(x2, pos2, invf, g, wd, qg, kvg, wuq, wuk, wuvt, ones_col):
    t = x2.shape[0]
    tm = TOKEN_TILE
    full = lambda a: pl.BlockSpec(a.shape, lambda i: (0, 0))
    vt_rows = wuvt.shape[0]
    return pl.pallas_call(
        _mla_proj_kernel,
        grid=(t // tm,),
        in_specs=[pl.BlockSpec((tm, D_MODEL), lambda i: (i, 0)),
                  pl.BlockSpec((None, 1, tm), lambda i: (i, 0, 0)),
                  full(invf), full(g), full(wd), full(qg), full(kvg), full(wuq), full(wuk),
                  full(wuvt), full(ones_col)],
        out_specs=[pl.BlockSpec((tm, MLA_HEADS * LANES), lambda i: (i, 0)),
                   pl.BlockSpec((tm, MLA_HEADS * LANES), lambda i: (i, 0)),
                   pl.BlockSpec((vt_rows, tm), lambda i: (0, i))],
        out_shape=[jax.ShapeDtypeStruct((t, MLA_HEADS * LANES), BF16),
                   jax.ShapeDtypeStruct((t, MLA_HEADS * LANES), BF16),
                   jax.ShapeDtypeStruct((vt_rows, t), BF16)],
        compiler_params=pltpu.CompilerParams(dimension_semantics=("parallel",),
                                             vmem_limit_bytes=VMEM_LIMIT),
        name="mla_proj",
    )(x2, pos2, invf, g, wd, qg, kvg, wuq, wuk, wuvt, ones_col)


def _attention_units(units, kq_fn, vt_fn, st_ref, n_val, seq, finish_fn):
    kc = KEY_CHUNK
    nchunk = seq // kc

    def scores_chunk(ui, c, m):
        k, q = kq_fn(units[ui], c)
        st = lax.dot_general(k, q, (((1,), (1,)), ((), ())), preferred_element_type=F32)
        st_ref[0, c * kc:(c + 1) * kc, :] = st
        mc = jnp.max(st, axis=0, keepdims=True)
        return mc if m is None else jnp.maximum(m, mc)

    m_next = None
    for c in range(nchunk):
        m_next = scores_chunk(0, c, m_next)
    for ui in range(len(units)):
        m_cur, m_next, r, pt_prev = m_next, None, None, None
        for c in range(nchunk + 1):
            pt_new = None
            if c < nchunk:
                pt_new = jnp.exp2(st_ref[0, c * kc:(c + 1) * kc, :] - m_cur).astype(BF16)
            if ui + 1 < len(units) and c < nchunk:
                m_next = scores_chunk(ui + 1, c, m_next)
            if pt_prev is not None:
                rc = jnp.dot(vt_fn(units[ui], c - 1), pt_prev, preferred_element_type=F32)
                r = rc if r is None else r + rc
            pt_prev = pt_new
        finish_fn(units[ui], r[:n_val] / r[n_val:n_val + 1])


def _mla_attn_kernel(q_ref, k_ref, vt_ref, o_ref, st_ref):
    seq = q_ref.shape[0]
    tq = st_ref.shape[2]
    rows = MLA_V + VT_ONES_ROWS
    kc = KEY_CHUNK
    nh = MLA_HEADS_PER_STEP
    units = [(hh, qi) for qi in range(seq // tq) for hh in range(nh)]
    done = {}

    def kq_fn(unit, c):
        hh, qi = unit
        blk = slice(hh * LANES, (hh + 1) * LANES)
        return k_ref[c * kc:(c + 1) * kc, blk], q_ref[qi * tq:(qi + 1) * tq, blk]

    def vt_fn(unit, c):
        hh, _ = unit
        return vt_ref[hh * rows:(hh + 1) * rows, c * kc:(c + 1) * kc]

    def finish_fn(unit, ot):
        hh, qi = unit
        done[hh] = ot
        if hh == nh - 1:
            o_ref[qi * tq:(qi + 1) * tq, :] = jnp.concatenate(
                [done[i] for i in range(nh)], axis=0).T.astype(BF16)

    _attention_units(units, kq_fn, vt_fn, st_ref, MLA_V, seq, finish_fn)


def _mla_attn(q, k, vt, batch, seq):
    t = q.shape[0]
    nh = MLA_HEADS_PER_STEP
    rows = nh * (MLA_V + VT_ONES_ROWS)
    return pl.pallas_call(
        _mla_attn_kernel,
        grid=(batch, MLA_HEADS // nh),
        in_specs=[pl.BlockSpec((seq, nh * LANES), lambda b, h: (b, h)),
                  pl.BlockSpec((seq, nh * LANES), lambda b, h: (b, h)),
                  pl.BlockSpec((rows, seq), lambda b, h: (h, b))],
        out_specs=pl.BlockSpec((seq, nh * MLA_V), lambda b, h: (b, h)),
        out_shape=jax.ShapeDtypeStruct((t, MLA_HEADS * MLA_V), BF16),
        scratch_shapes=[pltpu.VMEM((2, seq, Q_TILE), F32)],
        compiler_params=pltpu.CompilerParams(
            dimension_semantics=("parallel", "parallel"), vmem_limit_bytes=VMEM_LIMIT),
        name="mla_attn",
    )(q, k, vt)


def _diff_proj_kernel(x_ref, pos_ref, invf_ref, g_ref, w_ref, wvt_ref, ones_ref, q_ref, k_ref,
                      vt_ref):
    h = (_rms(x_ref[...]) * g_ref[...]).astype(BF16)
    qkv = jnp.dot(h, w_ref[...], preferred_element_type=F32)
    vt = lax.dot_general(wvt_ref[...], h, (((1,), (1,)), ((), ())), preferred_element_type=F32)
    vt_ref[...] = (vt + ones_ref[...]).astype(BF16)

    half = DIFF_ROPE // 2
    rest = DIFF_HEAD_DIM - DIFF_ROPE
    cos, sin = _rope_lane_tables(pos_ref, invf_ref, [True, True, rest, True, True, rest])
    lane = lax.broadcasted_iota(jnp.int32, cos.shape, 1)
    sin_lo = jnp.where(lane % DIFF_HEAD_DIM < half, -sin, 0.0)
    sin_hi = jnp.where(lane % DIFF_HEAD_DIM >= half, sin, 0.0)
    qs = (DIFF_HEAD_DIM ** -0.5) * LOG2E
    cos_q, sin_lo_q, sin_hi_q = cos * qs, sin_lo * qs, sin_hi * qs
    for hd in range(DIFF_HEADS):
        blk = slice(hd * LANES, (hd + 1) * LANES)
        kblk = slice(DIFF_QK_WIDTH + hd * LANES, DIFF_QK_WIDTH + (hd + 1) * LANES)
        q_ref[:, blk] = _apply_rope(qkv[:, blk], cos_q, sin_lo_q, sin_hi_q, half).astype(BF16)
        k_ref[:, blk] = _apply_rope(qkv[:, kblk], cos, sin_lo, sin_hi, half).astype(BF16)


def _diff_proj(x2, pos2, invf, g, w, wvt, ones_col):
    t = x2.shape[0]
    tm = TOKEN_TILE
    vt_rows = wvt.shape[0]
    full = lambda a: pl.BlockSpec(a.shape, lambda i: (0, 0))
    row = lambda n: pl.BlockSpec((tm, n), lambda i: (i, 0))
    return pl.pallas_call(
        _diff_proj_kernel,
        grid=(t // tm,),
        in_specs=[row(D_MODEL), pl.BlockSpec((None, 1, tm), lambda i: (i, 0, 0)), full(invf),
                  full(g), full(w), full(wvt), full(ones_col)],
        out_specs=[row(DIFF_QK_WIDTH), row(DIFF_QK_WIDTH),
                   pl.BlockSpec((vt_rows, tm), lambda i: (0, i))],
        out_shape=[jax.ShapeDtypeStruct((t, DIFF_QK_WIDTH), BF16),
                   jax.ShapeDtypeStruct((t, DIFF_QK_WIDTH), BF16),
                   jax.ShapeDtypeStruct((vt_rows, t), BF16)],
        compiler_params=pltpu.CompilerParams(dimension_semantics=("parallel",),
                                             vmem_limit_bytes=VMEM_LIMIT),
        name="diff_proj",
    )(x2, pos2, invf, g, w, wvt, ones_col)


def _diff_attn_kernel(lam_ref, q_ref, k_ref, vt_ref, o_ref, st_ref, *, lam_init):
    seq = q_ref.shape[0]
    tq = st_ref.shape[2]
    kc = KEY_CHUNK
    nh = DIFF_HEADS_PER_STEP
    rows = DIFF_V + VT_ONES_ROWS
    lam_p = lam_ref[...]
    lam = (jnp.exp(jnp.sum(lam_p[0:1] * lam_p[1:2], axis=-1, keepdims=True))
           - jnp.exp(jnp.sum(lam_p[2:3] * lam_p[3:4], axis=-1, keepdims=True)) + lam_init)
    lane = lax.broadcasted_iota(jnp.int32, (tq, LANES), 1)
    units = [(hh, comp, qi) for qi in range(seq // tq) for hh in range(nh) for comp in range(2)]
    masked_q = {}
    comps = {}
    done = {}

    def kq_fn(unit, c):
        hh, comp, qi = unit
        blk = slice(hh * LANES, (hh + 1) * LANES)
        if unit not in masked_q:
            q = q_ref[qi * tq:(qi + 1) * tq, blk]
            in_comp = (lane < DIFF_HEAD_DIM) if comp == 0 else (lane >= DIFF_HEAD_DIM)
            masked_q[unit] = jnp.where(in_comp, q, jnp.zeros_like(q))
        return k_ref[c * kc:(c + 1) * kc, blk], masked_q[unit]

    def vt_fn(unit, c):
        hh = unit[0]
        return vt_ref[hh * rows:(hh + 1) * rows, c * kc:(c + 1) * kc]

    def finish_fn(unit, ot):
        hh, comp, qi = unit
        comps[comp] = ot
        if comp == 1:
            o = comps[0] - lam * comps[1]
            done[hh] = (o * lax.rsqrt(jnp.mean(o * o, axis=0, keepdims=True) + NORM_EPS)
                        * (1.0 - lam_init))
            if hh == nh - 1:
                o_ref[qi * tq:(qi + 1) * tq, :] = jnp.concatenate(
                    [done[i] for i in range(nh)], axis=0).T.astype(BF16)

    _attention_units(units, kq_fn, vt_fn, st_ref, DIFF_V, seq, finish_fn)


def _diff_attn(lam_params, q, k, vt, batch, seq, lam_init):
    t = q.shape[0]
    nh = DIFF_HEADS_PER_STEP
    return pl.pallas_call(
        functools.partial(_diff_attn_kernel, lam_init=lam_init),
        grid=(batch, DIFF_HEADS // nh),
        in_specs=[pl.BlockSpec(lam_params.shape, lambda b, h: (0, 0)),
                  pl.BlockSpec((seq, nh * LANES), lambda b, h: (b, h)),
                  pl.BlockSpec((seq, nh * LANES), lambda b, h: (b, h)),
                  pl.BlockSpec((nh * (DIFF_V + VT_ONES_ROWS), seq), lambda b, h: (h, b))],
        out_specs=pl.BlockSpec((seq, nh * LANES), lambda b, h: (b, h)),
        out_shape=jax.ShapeDtypeStruct((t, DIFF_V_WIDTH), BF16),
        scratch_shapes=[pltpu.VMEM((2, seq, Q_TILE), F32)],
        compiler_params=pltpu.CompilerParams(
            dimension_semantics=("parallel", "parallel"), vmem_limit_bytes=VMEM_LIMIT),
        name="diff_attn",
    )(lam_params, q, k, vt)


def _ffn_kernel(o_ref, oprev_ref, onext_ref, x_ref, xprev_ref, xnext_ref, wo_ref, gf_ref, wg_ref,
                wv_ref, cwg_ref, cwv_ref, cbg_ref, cbv_ref, wdn_ref, fg_ref, out_ref, hext_ref,
                u_ref, xn_ref, acc_ref, *, nf, tiles_per_seq, apply_final_norm):
    s = pl.program_id(0)
    n_pairs = pl.num_programs(0) - 1
    up_pair = jnp.minimum(s, n_pairs - 1)
    up_tile, up_chunk = up_pair // nf, up_pair % nf
    chunk = jnp.maximum(s - 1, 0) % nf
    tm = x_ref.shape[0]
    hb = BF16_SUBLANES

    @pl.when(s == 0)
    def _():
        u_ref[1] = jnp.zeros(u_ref.shape[1:], F32)
        acc_ref[...] = jnp.zeros_like(acc_ref)

    @pl.when((up_chunk == 0) & (s < n_pairs))
    def _():
        at_start = (up_tile % tiles_per_seq) == 0
        at_end = (up_tile % tiles_per_seq) == tiles_per_seq - 1
        o_ext = jnp.concatenate([oprev_ref[...], o_ref[...], onext_ref[...]], axis=0)
        d = jnp.dot(o_ext, wo_ref[...], preferred_element_type=F32)
        g = gf_ref[...]
        xn = x_ref[...] + d[hb:hb + tm]
        xn_ref[...] = xn
        hext_ref[hb:hb + tm, :] = (_rms(xn) * g).astype(BF16)
        hprev = (_rms(xprev_ref[...] + d[:hb]) * g).astype(BF16)
        hnext = (_rms(xnext_ref[...] + d[hb + tm:]) * g).astype(BF16)
        hext_ref[0:hb, :] = jnp.where(at_start, jnp.zeros_like(hprev), hprev)
        hext_ref[hb + tm:, :] = jnp.where(at_end, jnp.zeros_like(hnext), hnext)

    @pl.when((s > 0) & (chunk == 0))
    def _():
        acc_ref[...] = xn_ref[...]

    prev = (s + 1) % 2

    def conv(j, cw_ref, cb_ref):
        cols = slice((j % 2) * LANES, (j % 2 + 1) * LANES)
        cw = cw_ref[...]
        return (u_ref[prev, j, hb - 1:hb - 1 + tm, :] * cw[0:1, cols]
                + u_ref[prev, j, hb:hb + tm, :] * cw[1:2, cols]
                + u_ref[prev, j, hb + 1:hb + 1 + tm, :] * cw[2:3, cols] + cb_ref[:, cols])

    gate = jnp.concatenate([conv(0, cwg_ref, cbg_ref), conv(1, cwg_ref, cbg_ref)], axis=1)
    val = jnp.concatenate([conv(2, cwv_ref, cbv_ref), conv(3, cwv_ref, cbv_ref)], axis=1)
    act = (gate * jax.nn.sigmoid(gate) * val).astype(BF16)
    acc_ref[...] += jnp.dot(act, wdn_ref[...], preferred_element_type=F32)

    hx = hext_ref[...]
    ug = jnp.dot(hx, wg_ref[...], preferred_element_type=F32)
    uv = jnp.dot(hx, wv_ref[...], preferred_element_type=F32)
    for j in range(2):
        u_ref[s % 2, j] = ug[:, j * LANES:(j + 1) * LANES]
        u_ref[s % 2, 2 + j] = uv[:, j * LANES:(j + 1) * LANES]

    @pl.when((s > 0) & (chunk == nf - 1))
    def _():
        y = acc_ref[...]
        if apply_final_norm:
            y = _rms(y) * fg_ref[...]
        out_ref[...] = y


def _ffn(o, x2, wo, gf, wup, cw, cb, wdn, fg, seq, apply_final_norm):
    t = x2.shape[0]
    tm = FFN_TILE
    fc = FF_CHUNK
    hb = BF16_SUBLANES
    nf = D_FF // fc
    n_pairs = (t // tm) * nf
    halo_per_tile = tm // hb
    last_halo = t // hb - 1
    up_tile = lambda s: jnp.minimum(s, n_pairs - 1) // nf
    up_chunk = lambda s: jnp.minimum(s, n_pairs - 1) % nf
    tile = lambda s: jnp.maximum(s - 1, 0) // nf
    chunk = lambda s: jnp.maximum(s - 1, 0) % nf
    main = pl.BlockSpec((tm, D_MODEL), lambda s: (up_tile(s), 0))
    prev_halo = pl.BlockSpec((hb, D_MODEL),
                             lambda s: (jnp.maximum(up_tile(s) * halo_per_tile - 1, 0), 0))
    next_halo = pl.BlockSpec((hb, D_MODEL),
                             lambda s: (jnp.minimum((up_tile(s) + 1) * halo_per_tile, last_halo), 0))
    return pl.pallas_call(
        functools.partial(_ffn_kernel, nf=nf, tiles_per_seq=seq // tm,
                          apply_final_norm=apply_final_norm),
        grid=(n_pairs + 1,),
        in_specs=[main, prev_halo, next_halo, main, prev_halo, next_halo,
                  pl.BlockSpec((D_MODEL, D_MODEL), lambda s: (0, 0)),
                  pl.BlockSpec((1, D_MODEL), lambda s: (0, 0)),
                  pl.BlockSpec((D_MODEL, fc), lambda s: (0, up_chunk(s))),
                  pl.BlockSpec((D_MODEL, fc), lambda s: (0, nf + up_chunk(s))),
                  pl.BlockSpec((CONV_WIDTH, fc), lambda s: (0, chunk(s))),
                  pl.BlockSpec((CONV_WIDTH, fc), lambda s: (0, nf + chunk(s))),
                  pl.BlockSpec((1, fc), lambda s: (0, chunk(s))),
                  pl.BlockSpec((1, fc), lambda s: (0, nf + chunk(s))),
                  pl.BlockSpec((fc, D_MODEL), lambda s: (chunk(s), 0)),
                  pl.BlockSpec((1, D_MODEL), lambda s: (0, 0))],
        out_specs=pl.BlockSpec((tm, D_MODEL), lambda s: (tile(s), 0)),
        out_shape=jax.ShapeDtypeStruct((t, D_MODEL), F32),
        scratch_shapes=[pltpu.VMEM((tm + 2 * hb, D_MODEL), BF16),
                        pltpu.VMEM((2, 4, tm + 2 * hb, LANES), F32),
                        pltpu.VMEM((tm, D_MODEL), F32),
                        pltpu.VMEM((tm, D_MODEL), F32)],
        compiler_params=pltpu.CompilerParams(dimension_semantics=("arbitrary",),
                                             vmem_limit_bytes=FFN_VMEM_LIMIT),
        name="conv_ffn",
    )(o, o, o, x2, x2, x2, wo, gf, wup, wup, cw, cw, cb, cb, wdn, fg)


def _inv_freq_col(rot_dim):
    return (ROPE_THETA ** (-jnp.arange(0, rot_dim, 2, dtype=F32) / rot_dim))[:, None]


def _pad_heads(w, heads, width):
    w = w.reshape(w.shape[0], heads, width)
    return jnp.pad(w, ((0, 0), (0, 0), (0, LANES - width))).reshape(w.shape[0], heads * LANES)


def _value_weight_t(w, heads, width):
    wt = w.T.reshape(heads, width, w.shape[0])
    wt = jnp.pad(wt, ((0, 0), (0, VT_ONES_ROWS), (0, 0))).reshape(-1, w.shape[0])
    ones = jnp.pad(jnp.zeros((heads, width, 1), F32), ((0, 0), (0, VT_ONES_ROWS), (0, 0)),
                   constant_values=1.0).reshape(-1, 1)
    return wt.astype(BF16), ones


def kernel(x, positions, attn_norm_g, ffn_norm_g, final_norm_g, mla_w_down, mla_q_norm_g,
           mla_kv_norm_g, mla_w_uq, mla_w_ukv, mla_w_o, diff_w_qkv, diff_lam_q1, diff_lam_k1,
           diff_lam_q2, diff_lam_k2, diff_w_o, ffn_w_up, ffn_conv_w, ffn_conv_b, ffn_w_down):
    batch, seq, _ = x.shape
    t = batch * seq
    x2 = x.reshape(t, D_MODEL)
    pos2 = positions.reshape(t // TOKEN_TILE, 1, TOKEN_TILE)
    row = lambda a: a.reshape(1, -1)

    def out_proj_ffn(o, x2, w_o, layer, final):
        return _ffn(o, x2, w_o.astype(BF16), row(ffn_norm_g[layer]),
                    ffn_w_up[layer].astype(BF16), ffn_conv_w[layer], row(ffn_conv_b[layer]),
                    ffn_w_down[layer].astype(BF16), row(final_norm_g), seq, final)

    wd = mla_w_down[0]
    k_rope_cols = jnp.pad(wd[:, MLA_Q_LORA + MLA_KV_LORA:], ((0, 0), (MLA_NOPE, LANES - MLA_QK)))
    wd_p = jnp.concatenate([wd[:, :MLA_Q_LORA + MLA_KV_LORA], k_rope_cols], axis=1).astype(BF16)
    wukv = mla_w_ukv[0].reshape(MLA_KV_LORA, MLA_HEADS, MLA_NOPE + MLA_V)
    wuk_p = _pad_heads(wukv[:, :, :MLA_NOPE].reshape(MLA_KV_LORA, -1), MLA_HEADS, MLA_NOPE)
    wuvt, ones_col = _value_weight_t(wukv[:, :, MLA_NOPE:].reshape(MLA_KV_LORA, -1), MLA_HEADS,
                                     MLA_V)
    q, k, vt = _mla_proj(x2, pos2, _inv_freq_col(MLA_ROPE), row(attn_norm_g[0]),
                         wd_p, row(mla_q_norm_g[0]), row(mla_kv_norm_g[0]),
                         _pad_heads(mla_w_uq[0], MLA_HEADS, MLA_QK).astype(BF16),
                         wuk_p.astype(BF16), wuvt, ones_col)
    o = _mla_attn(q, k, vt, batch, seq)
    x2 = out_proj_ffn(o, x2, mla_w_o[0], 0, False)

    lam_init = 0.8 - 0.6 * math.exp(-0.3 * 1)
    wvt, ones_col = _value_weight_t(diff_w_qkv[0][:, 2 * DIFF_QK_WIDTH:], DIFF_HEADS, DIFF_V)
    q, k, vt = _diff_proj(x2, pos2, _inv_freq_col(DIFF_ROPE),
                          row(attn_norm_g[1]), diff_w_qkv[0][:, :2 * DIFF_QK_WIDTH].astype(BF16),
                          wvt, ones_col)
    lam_params = jnp.stack([diff_lam_q1[0], diff_lam_k1[0], diff_lam_q2[0], diff_lam_k2[0]])
    o = _diff_attn(lam_params, q, k, vt, batch, seq, lam_init)
    x2 = out_proj_ffn(o, x2, diff_w_o[0], 1, True)
    return x2.reshape(batch, seq, D_MODEL)
```
